```python
import math
import jax, jax.numpy as jnp
from jax import lax
import numpy as np

D_MODEL = 1024
BATCH = 16
SEQ = 4096
DEPTH = 1

HEAD_DIM = 64
DIL_GROUPS = ((128, 1), (512, 4), (2048, 16))
N_DIL = len(DIL_GROUPS)
DIL_HEADS_PER_GROUP = 4
DIL_HEADS = N_DIL * DIL_HEADS_PER_GROUP
DIL_WIDTH = DIL_HEADS * HEAD_DIM
DIL_OUT = DIL_HEADS_PER_GROUP * HEAD_DIM
FOX_HEADS = 8
FOX_WIDTH = FOX_HEADS * HEAD_DIM
BLOCK = 128
D_FF = 2816
N_MOD = 9
RMS_EPS = 1e-6
ALIBI_MAX_BIAS = 8.0
NEG_INF = -1e30
IN_SIZES = (DIL_WIDTH, DIL_WIDTH, DIL_WIDTH, FOX_WIDTH, FOX_WIDTH, FOX_WIDTH,
            FOX_HEADS, D_MODEL, D_MODEL)
N_IN = sum(IN_SIZES)

kernel_name = "hybrid_dilated_fox_macaron_block"


def rms_norm(x, g):
    xf = x.astype(jnp.float32)
    y = xf * lax.rsqrt(jnp.mean(xf * xf, axis=-1, keepdims=True) + RMS_EPS)
    return (y * g.astype(jnp.float32)).astype(x.dtype)


def modulate(h, shift, scale):
    return h * (1 + scale[:, None, :]) + shift[:, None, :]


def swiglu(h, w_gate, w_up, w_down):
    return (jax.nn.silu(h @ w_gate) * (h @ w_up)) @ w_down


def alibi_slopes():
    s = np.array([2.0 ** (-ALIBI_MAX_BIAS * (i + 1) / DIL_HEADS) for i in range(DIL_HEADS)],
                 dtype=np.float32)
    return jnp.asarray(s.reshape(N_DIL, DIL_HEADS_PER_GROUP))


def dilated_window_attention(q, k, v, slopes, window, dilation):
    bsz, seq, nh, e = q.shape
    n_back = window // dilation
    span = dilation * BLOCK
    s_pad = -(-seq // span) * span
    sub_len = s_pad // dilation
    nb = sub_len // BLOCK

    def to_sub(t):
        t = jnp.pad(t, ((0, 0), (0, s_pad - seq), (0, 0), (0, 0)))
        t = t.reshape(bsz, sub_len, dilation, nh, e).transpose(0, 2, 3, 1, 4)
        return t.reshape(bsz, dilation, nh, nb, BLOCK, e)

    def with_prev(t):
        prev = jnp.pad(t[:, :, :, :-1], ((0, 0), (0, 0), (0, 0), (1, 0), (0, 0), (0, 0)))
        return jnp.concatenate([prev, t], axis=4)

    qs = to_sub(q)
    kb, vb = with_prev(to_sub(k)), with_prev(to_sub(v))
    scores = jnp.einsum('bdhnqe,bdhnke->bdhnqk', qs, kb).astype(jnp.float32) / math.sqrt(e)
    qi = jnp.arange(BLOCK)[:, None]
    ki = jnp.arange(2 * BLOCK)[None, :]
    rel = BLOCK + qi - ki
    blk = jnp.arange(nb)[:, None, None]
    valid = (rel >= 0) & (rel <= n_back) & ((blk > 0) | (ki >= BLOCK))
    bias = -slopes.astype(jnp.float32)[:, None, None] * (rel * dilation).astype(jnp.float32)
    scores = jnp.where(valid, scores + bias[:, None], NEG_INF)
    lse = jax.nn.logsumexp(scores, axis=-1)
    p = jnp.exp(scores - lse[..., None])
    o = jnp.einsum('bdhnqk,bdhnke->bdhnqe', p.astype(v.dtype), vb)

    def from_sub(t):
        t = t.reshape(bsz, dilation, nh, sub_len, -1).transpose(0, 3, 1, 2, 4)
        return t.reshape(bsz, s_pad, nh, -1)[:, :seq]

    return from_sub(o), from_sub(lse[..., None])[..., 0]


def forgetting_attention(q, k, v, f_logit):
    bsz, seq, nh, e = q.shape
    nb = seq // BLOCK
    cum = jnp.cumsum(jax.nn.log_sigmoid(f_logit.astype(jnp.float32)), axis=1).transpose(0, 2, 1)
    kt = k.transpose(0, 2, 1, 3)
    vt = v.transpose(0, 2, 1, 3)
    q_blocks = q.reshape(bsz, nb, BLOCK, nh, e).transpose(1, 0, 3, 2, 4)
    cum_blocks = cum.reshape(bsz, nh, nb, BLOCK).transpose(2, 0, 1, 3)
    key_pos = jnp.arange(seq)
    scale = 1.0 / math.sqrt(e)

    def one_block(args):
        qb, cb, b = args
        s = jnp.einsum('bhqe,bhke->bhqk', qb, kt).astype(jnp.float32) * scale
        s = s + cb[..., None] - cum[:, :, None, :]
        q_pos = b * BLOCK + jnp.arange(BLOCK)
        s = jnp.where(key_pos[None, :] <= q_pos[:, None], s, NEG_INF)
        p = jax.nn.softmax(s, axis=-1)
        return jnp.einsum('bhqk,bhke->bhqe', p.astype(vt.dtype), vt)

    out = lax.map(one_block, (q_blocks, cum_blocks, jnp.arange(nb)))
    return out.transpose(1, 0, 3, 2, 4).reshape(bsz, seq, nh * e)


def hybrid_mixer(h, w_in, forget_bias, w_branch_a, w_branch_b, w_out):
    bsz, seq, _ = h.shape
    proj = h @ w_in
    offsets = [int(o) for o in np.cumsum(IN_SIZES)[:-1]]
    qa, ka, va, qb, kb, vb, f_logit, ga, gb = jnp.split(proj, offsets, axis=-1)

    qa = qa.reshape(bsz, seq, N_DIL, DIL_HEADS_PER_GROUP, HEAD_DIM)
    ka = ka.reshape(bsz, seq, N_DIL, DIL_HEADS_PER_GROUP, HEAD_DIM)
    va = va.reshape(bsz, seq, N_DIL, DIL_HEADS_PER_GROUP, HEAD_DIM)
    slopes = alibi_slopes()
    outs, lses = [], []
    for g, (window, dilation) in enumerate(DIL_GROUPS):
        o, lse = dilated_window_attention(qa[:, :, g], ka[:, :, g], va[:, :, g],
                                          slopes[g], window, dilation)
        outs.append(o)
        lses.append(lse)
    weights = jax.nn.softmax(jnp.stack(lses, axis=0), axis=0)
    y_a = jnp.sum(weights[..., None].astype(h.dtype) * jnp.stack(outs, axis=0), axis=0)
    y_a = y_a.reshape(bsz, seq, DIL_OUT) @ w_branch_a

    f_logit = f_logit + forget_bias
    y_b = forgetting_attention(qb.reshape(bsz, seq, FOX_HEADS, HEAD_DIM),
                               kb.reshape(bsz, seq, FOX_HEADS, HEAD_DIM),
                               vb.reshape(bsz, seq, FOX_HEADS, HEAD_DIM), f_logit)
    y_b = y_b @ w_branch_b

    merged = jax.nn.sigmoid(ga) * y_a + jax.nn.sigmoid(gb) * y_b
    return merged @ w_out


def setup_inputs(seed: int = 0) -> dict:
    key = jax.random.key(seed)
    ks = jax.random.split(key, 20)
    f32 = jnp.float32

    def w(k, shape, fan_in):
        return jax.random.normal(k, shape, f32) * fan_in ** -0.5

    def gain(k, shape):
        return 1.0 + 0.02 * jax.random.normal(k, shape, f32)

    return {
        "x": jax.random.normal(ks[0], (BATCH, SEQ, D_MODEL), f32),
        "c": jax.random.normal(ks[1], (BATCH, D_MODEL), f32),
        "ada_w": w(ks[2], (DEPTH, D_MODEL, N_MOD * D_MODEL), D_MODEL),
        "ada_b": 0.02 * jax.random.normal(ks[3], (DEPTH, N_MOD * D_MODEL), f32),
        "norm_ffn1": gain(ks[4], (DEPTH, D_MODEL)),
        "ffn1_w_gate": w(ks[5], (DEPTH, D_MODEL, D_FF), D_MODEL),
        "ffn1_w_up": w(ks[6], (DEPTH, D_MODEL, D_FF), D_MODEL),
        "ffn1_w_down": w(ks[7], (DEPTH, D_FF, D_MODEL), D_FF),
        "norm_mix": gain(ks[8], (DEPTH, D_MODEL)),
        "w_in": w(ks[9], (DEPTH, D_MODEL, N_IN), D_MODEL),
        "forget_bias": jnp.linspace(1.0, 6.0, FOX_HEADS, dtype=f32)[None, :]
                       + 0.1 * jax.random.normal(ks[10], (DEPTH, FOX_HEADS), f32),
        "w_branch_a": w(ks[11], (DEPTH, DIL_OUT, D_MODEL), DIL_OUT),
        "w_branch_b": w(ks[12], (DEPTH, FOX_WIDTH, D_MODEL), FOX_WIDTH),
        "w_out": w(ks[13], (DEPTH, D_MODEL, D_MODEL), D_MODEL),
        "norm_ffn2": gain(ks[14], (DEPTH, D_MODEL)),
        "ffn2_w_gate": w(ks[15], (DEPTH, D_MODEL, D_FF), D_MODEL),
        "ffn2_w_up": w(ks[16], (DEPTH, D_MODEL, D_FF), D_MODEL),
        "ffn2_w_down": w(ks[17], (DEPTH, D_FF, D_MODEL), D_FF),
        "norm_final": gain(ks[18], (D_MODEL,)),
    }


def reference(x, c, ada_w, ada_b, norm_ffn1, ffn1_w_gate, ffn1_w_up, ffn1_w_down,
              norm_mix, w_in, forget_bias, w_branch_a, w_branch_b, w_out,
              norm_ffn2, ffn2_w_gate, ffn2_w_up, ffn2_w_down, norm_final):
    c_act = jax.nn.silu(c)
    for layer in range(DEPTH):
        mod = c_act @ ada_w[layer] + ada_b[layer]
        (sh1, sc1, gt1, sh2, sc2, gt2, sh3, sc3, gt3) = jnp.split(mod, N_MOD, axis=-1)
        h = modulate(rms_norm(x, norm_ffn1[layer]), sh1, sc1)
        x = x + 0.5 * gt1[:, None, :] * swiglu(h, ffn1_w_gate[layer], ffn1_w_up[layer],
                                               ffn1_w_down[layer])
        h = modulate(rms_norm(x, norm_mix[layer]), sh2, sc2)
        x = x + gt2[:, None, :] * hybrid_mixer(h, w_in[layer], forget_bias[layer],
                                               w_branch_a[layer], w_branch_b[layer],
                                               w_out[layer])
        h = modulate(rms_norm(x, norm_ffn2[layer]), sh3, sc3)
        x = x + 0.5 * gt3[:, None, :] * swiglu(h, ffn2_w_gate[layer], ffn2_w_up[layer],
                                               ffn2_w_down[layer])
    return rms_norm(x, norm_final)
```

```python
import functools
import math

import numpy as np
import jax
import jax.numpy as jnp
from jax import lax
from jax.experimental import pallas as pl
from jax.experimental.pallas import tpu as pltpu

D_MODEL = 1024
HEAD_DIM = 64
DIL_GROUPS = ((128, 1), (512, 4), (2048, 16))
N_DIL = len(DIL_GROUPS)
DIL_HEADS_PER_GROUP = 4
DIL_HEADS = N_DIL * DIL_HEADS_PER_GROUP
DIL_WIDTH = DIL_HEADS * HEAD_DIM
DIL_OUT = DIL_HEADS_PER_GROUP * HEAD_DIM
FOX_HEADS = 8
FOX_WIDTH = FOX_HEADS * HEAD_DIM
BLOCK = 128
N_MOD = 9
RMS_EPS = 1e-6
ALIBI_MAX_BIAS = 8.0
NEG_INF = -1e30
IN_SIZES = (DIL_WIDTH, DIL_WIDTH, DIL_WIDTH, FOX_WIDTH, FOX_WIDTH, FOX_WIDTH,
            FOX_HEADS, D_MODEL, D_MODEL)

LANES = 128
VMEM_LIMIT_BYTES = 56 * 1024 * 1024

FFN_ROWS = 512
FFN_CHUNK = 256
PROJ_ROWS = 512
PROJ_CHUNK = 256
FOX_BLOCK = 512
ADA_COLS = 1024

_F32 = jnp.float32
_BF16 = jnp.bfloat16


def _const_spec(shape):
    zeros = (0,) * len(shape)
    return pl.BlockSpec(shape, lambda *_: zeros, pipeline_mode=pl.Buffered(1))


def _params(*semantics):
    return pltpu.CompilerParams(dimension_semantics=semantics,
                                vmem_limit_bytes=VMEM_LIMIT_BYTES)


def _modulated_norm(x, gain, shift, scale):
    ms = jnp.mean(x * x, axis=-1, keepdims=True)
    y = x * lax.rsqrt(ms + RMS_EPS) * gain
    return y * (1.0 + scale) + shift


def _ada_kernel(c_ref, w_ref, b_ref, o_ref):
    c = c_ref[...]
    act = c * jax.nn.sigmoid(c)
    o_ref[...] = jnp.dot(act, w_ref[...], preferred_element_type=_F32,
                         precision=lax.Precision.HIGHEST) + b_ref[...]


def _ada_modulation(c, w, b):
    bsz, d = c.shape
    n = w.shape[1]
    return pl.pallas_call(
        _ada_kernel,
        out_shape=jax.ShapeDtypeStruct((bsz, n), _F32),
        grid=(n // ADA_COLS,),
        in_specs=[pl.BlockSpec((bsz, d), lambda j: (0, 0)),
                  pl.BlockSpec((d, ADA_COLS), lambda j: (0, j)),
                  pl.BlockSpec((1, ADA_COLS), lambda j: (0, j))],
        out_specs=pl.BlockSpec((bsz, ADA_COLS), lambda j: (0, j)),
        compiler_params=_params("arbitrary"),
        name="ada_modulation",
    )(c, w, b.reshape(1, n))


def _ffn_kernel(*refs, n_chunks, final_norm):
    if final_norm:
        (x_ref, sh_ref, sc_ref, gt_ref, g_ref, wg_ref, wu_ref, wd_ref, gf_ref,
         o_ref, h_scr, acc_scr) = refs
    else:
        (x_ref, sh_ref, sc_ref, gt_ref, g_ref, wg_ref, wu_ref, wd_ref,
         o_ref, h_scr, acc_scr) = refs
    h = _modulated_norm(x_ref[...], g_ref[...], sh_ref[...], sc_ref[...])
    h_scr[...] = h.astype(_BF16)
    acc_scr[...] = jnp.zeros_like(acc_scr)

    def body(j, carry):
        hb = h_scr[...]
        g = jnp.dot(hb, wg_ref[j], preferred_element_type=_F32)
        u = jnp.dot(hb, wu_ref[j], preferred_element_type=_F32)
        a = (g * jax.nn.sigmoid(g) * u).astype(_BF16)
        acc_scr[...] += jnp.dot(a, wd_ref[j], preferred_element_type=_F32)
        return carry

    lax.fori_loop(0, n_chunks, body, 0)
    out = x_ref[...] + 0.5 * gt_ref[...] * acc_scr[...]
    if final_norm:
        ms = jnp.mean(out * out, axis=-1, keepdims=True)
        out = out * lax.rsqrt(ms + RMS_EPS) * gf_ref[...]
    o_ref[...] = out


def _ffn(x, shift, scale, gate, gain, w_gate, w_up, w_down, final_gain=None):
    bsz, seq, d = x.shape
    d_ff = w_gate.shape[1]
    n_chunks = d_ff // FFN_CHUNK
    wg = w_gate.astype(_BF16).reshape(d, n_chunks, FFN_CHUNK).transpose(1, 0, 2)
    wu = w_up.astype(_BF16).reshape(d, n_chunks, FFN_CHUNK).transpose(1, 0, 2)
    wd = w_down.astype(_BF16).reshape(n_chunks, FFN_CHUNK, d)
    row_spec = pl.BlockSpec((None, FFN_ROWS, d), lambda b, i: (b, i, 0))
    mod_spec = pl.BlockSpec((None, 1, d), lambda b, i: (b, 0, 0))
    in_specs = [row_spec, mod_spec, mod_spec, mod_spec, _const_spec((1, d)),
                _const_spec(wg.shape), _const_spec(wu.shape), _const_spec(wd.shape)]
    args = [x, shift, scale, gate, gain.reshape(1, d), wg, wu, wd]
    if final_gain is not None:
        in_specs.append(_const_spec((1, d)))
        args.append(final_gain.reshape(1, d))
    return pl.pallas_call(
        functools.partial(_ffn_kernel, n_chunks=n_chunks, final_norm=final_gain is not None),
        out_shape=jax.ShapeDtypeStruct(x.shape, _F32),
        grid=(bsz, seq // FFN_ROWS),
        in_specs=in_specs,
        out_specs=row_spec,
        scratch_shapes=[pltpu.VMEM((FFN_ROWS, d), _BF16), pltpu.VMEM((FFN_ROWS, d), _F32)],
        compiler_params=_params("parallel", "parallel"),
        name="ffn_final" if final_gain is not None else "ffn",
    )(*args)


def _lane_cumsum(x):
    n = x.shape[-1]
    lane = lax.broadcasted_iota(jnp.int32, x.shape, x.ndim - 1)
    shift = 1
    while shift < n:
        x = x + jnp.where(lane >= shift, pltpu.roll(x, shift, x.ndim - 1), 0.0)
        shift *= 2
    return x


def _proj_kernel(x_ref, sh_ref, sc_ref, g_ref, wa_ref, wb_ref, wgate_ref, wf_ref, fb_ref,
                 oa_ref, ob_ref, og_ref, cum_ref, h_scr, carry_scr):
    h_scr[...] = _modulated_norm(x_ref[...], g_ref[...], sh_ref[...], sc_ref[...]).astype(_BF16)

    def project(w_ref, o_ref, act):
        for c in range(w_ref.shape[1] // PROJ_CHUNK):
            cols = slice(c * PROJ_CHUNK, (c + 1) * PROJ_CHUNK)
            y = jnp.dot(h_scr[...], w_ref[:, cols], preferred_element_type=_F32)
            o_ref[:, cols] = act(y).astype(o_ref.dtype)

    project(wa_ref, oa_ref, lambda y: y)
    project(wb_ref, ob_ref, lambda y: y)
    project(wgate_ref, og_ref, jax.nn.sigmoid)

    @pl.when(pl.program_id(1) == 0)
    def _():
        carry_scr[...] = jnp.zeros_like(carry_scr)

    f = lax.dot_general(wf_ref[...], h_scr[...], (((1,), (1,)), ((), ())),
                        preferred_element_type=_F32) + fb_ref[...]
    log_sig = jnp.minimum(f, 0.0) - jnp.log1p(jnp.exp(-jnp.abs(f)))
    cum = _lane_cumsum(log_sig) + carry_scr[...][:, 0:1]
    cum_ref[...] = cum
    carry_scr[...] = jnp.broadcast_to(cum[:, -1:], carry_scr.shape)


def _mixer_projection(x, shift, scale, gain, w_a, w_b, w_gates, w_f_t, f_bias):
    bsz, seq, d = x.shape
    row = lambda n: pl.BlockSpec((None, PROJ_ROWS, n), lambda b, i: (b, i, 0))
    mod_spec = pl.BlockSpec((None, 1, d), lambda b, i: (b, 0, 0))
    return pl.pallas_call(
        _proj_kernel,
        out_shape=(jax.ShapeDtypeStruct((bsz, seq, w_a.shape[1]), _BF16),
                   jax.ShapeDtypeStruct((bsz, seq, w_b.shape[1]), _BF16),
                   jax.ShapeDtypeStruct((bsz, seq, w_gates.shape[1]), _BF16),
                   jax.ShapeDtypeStruct((bsz, FOX_HEADS, seq), _F32)),
        grid=(bsz, seq // PROJ_ROWS),
        in_specs=[row(d), mod_spec, mod_spec, _const_spec((1, d)),
                  _const_spec(w_a.shape), _const_spec(w_b.shape), _const_spec(w_gates.shape),
                  _const_spec(w_f_t.shape), _const_spec((FOX_HEADS, 1))],
        out_specs=(row(w_a.shape[1]), row(w_b.shape[1]), row(w_gates.shape[1]),
                   pl.BlockSpec((None, FOX_HEADS, PROJ_ROWS), lambda b, i: (b, 0, i))),
        scratch_shapes=[pltpu.VMEM((PROJ_ROWS, d), _BF16), pltpu.VMEM((FOX_HEADS, LANES), _F32)],
        compiler_params=_params("parallel", "arbitrary"),
        name="mixer_projection",
    )(x, shift, scale, gain.reshape(1, d), w_a, w_b, w_gates, w_f_t, f_bias)


def _dil_kernel(q_ref, kp_ref, kc_ref, vp_ref, vc_ref, bias_ref, o_ref, lse_ref,
                k_scr, v_scr, *, rows):
    k_scr[0:BLOCK, :] = kp_ref[...]
    k_scr[BLOCK:, :] = kc_ref[...]
    v_scr[0:BLOCK, :] = vp_ref[...]
    v_scr[BLOCK:, :] = vc_ref[...]
    first_tile = pl.program_id(2) == 0
    lane = lax.broadcasted_iota(jnp.int32, (BLOCK, DIL_OUT), 1)
    key = lax.broadcasted_iota(jnp.int32, (BLOCK, 2 * BLOCK), 1)
    for c in range(rows // BLOCK):
        q = q_ref[c * BLOCK:(c + 1) * BLOCK, :]
        kc = k_scr[c * BLOCK:(c + 2) * BLOCK, :]
        vc = v_scr[c * BLOCK:(c + 2) * BLOCK, :]
        o = jnp.zeros((BLOCK, DIL_OUT), _F32)
        lse = jnp.zeros((BLOCK, DIL_OUT), _F32)
        for h in range(DIL_HEADS_PER_GROUP):
            head = (lane >= h * HEAD_DIM) & (lane < (h + 1) * HEAD_DIM)
            qh = jnp.where(head, q, jnp.zeros_like(q))
            s = lax.dot_general(qh, kc, (((1,), (1,)), ((), ())),
                                preferred_element_type=_F32) + bias_ref[h]
            if c == 0:
                s = jnp.where(first_tile & (key < BLOCK), NEG_INF, s)
            m = jnp.max(s, axis=1, keepdims=True)
            p = jnp.exp(s - m)
            l = jnp.sum(p, axis=1, keepdims=True)
            pv = jnp.dot(p.astype(_BF16), vc, preferred_element_type=_F32)
            o = jnp.where(head, pv / l, o)
            lse = jnp.where(head, m + jnp.log(l), lse)
        o_ref[c * BLOCK:(c + 1) * BLOCK, :] = o.astype(o_ref.dtype)
        lse_ref[c * BLOCK:(c + 1) * BLOCK, :] = lse


def _dilated_bias(group):
    window, dilation = DIL_GROUPS[group]
    n_back = window // dilation
    slopes = np.array([2.0 ** (-ALIBI_MAX_BIAS * (i + 1) / DIL_HEADS) for i in range(DIL_HEADS)],
                      dtype=np.float32).reshape(N_DIL, DIL_HEADS_PER_GROUP)[group]
    qi = np.arange(BLOCK)[:, None]
    ki = np.arange(2 * BLOCK)[None, :]
    rel = BLOCK + qi - ki
    valid = (rel >= 0) & (rel <= n_back)
    bias = -slopes[:, None, None] * (rel * dilation).astype(np.float32)[None]
    return jnp.asarray(np.where(valid[None], bias, np.float32(NEG_INF)).astype(np.float32))


def _dilated_attention(qkv, group):
    bsz, seq, width = qkv.shape
    _, dilation = DIL_GROUPS[group]
    sub_len = seq // dilation
    rows = min(sub_len, 4 * BLOCK)
    n_tiles = sub_len // rows
    col_blocks = width // DIL_OUT
    blocks_per_tile = rows // BLOCK
    view = qkv.reshape(bsz, sub_len, dilation * width)
    n_q = N_DIL * 0 + group
    n_k = N_DIL * 1 + group
    n_v = N_DIL * 2 + group

    def cur(n):
        return pl.BlockSpec((None, rows, DIL_OUT), lambda b, r, i: (b, i, r * col_blocks + n))

    def prev(n):
        return pl.BlockSpec((None, BLOCK, DIL_OUT),
                            lambda b, r, i: (b, jnp.maximum(i * blocks_per_tile - 1, 0),
                                             r * col_blocks + n))

    out_spec = pl.BlockSpec((None, rows, DIL_OUT), lambda b, r, i: (b, i, r))
    o, lse = pl.pallas_call(
        functools.partial(_dil_kernel, rows=rows),
        out_shape=(jax.ShapeDtypeStruct((bsz, sub_len, dilation * DIL_OUT), _BF16),
                   jax.ShapeDtypeStruct((bsz, sub_len, dilation * DIL_OUT), _F32)),
        grid=(bsz, dilation, n_tiles),
        in_specs=[cur(n_q), prev(n_k), cur(n_k), prev(n_v), cur(n_v),
                  _const_spec((DIL_HEADS_PER_GROUP, BLOCK, 2 * BLOCK))],
        out_specs=(out_spec, out_spec),
        scratch_shapes=[pltpu.VMEM((rows + BLOCK, DIL_OUT), _BF16),
                        pltpu.VMEM((rows + BLOCK, DIL_OUT), _BF16)],
        compiler_params=_params("parallel", "parallel", "parallel"),
        name=f"dilated_attention_d{dilation}",
    )(view, view, view, view, view, _dilated_bias(group))
    return o.reshape(bsz, seq, DIL_OUT), lse.reshape(bsz, seq, DIL_OUT)


def _fox_kernel(q_ref, k_ref, v_ref, cum_ref, o_ref, acc_scr, m_scr, l_scr, *, blk):
    i = pl.program_id(2)
    lane = lax.broadcasted_iota(jnp.int32, (blk, LANES), 1)
    low = lane < HEAD_DIM
    q = q_ref[...]
    q_heads = (jnp.where(low, q, jnp.zeros_like(q)), jnp.where(low, jnp.zeros_like(q), q))
    acc_scr[...] = jnp.zeros_like(acc_scr)
    m_scr[...] = jnp.full_like(m_scr, NEG_INF)
    l_scr[...] = jnp.zeros_like(l_scr)
    row = lax.broadcasted_iota(jnp.int32, (blk, blk), 0)
    col = lax.broadcasted_iota(jnp.int32, (blk, blk), 1)

    def step(j, diagonal):
        start = pl.multiple_of(j * blk, blk)
        kb = k_ref[pl.ds(start, blk), :]
        vb = v_ref[pl.ds(start, blk), :]
        decay = cum_ref[j]
        alphas, pvs = [], []
        for hd in range(2):
            s = lax.dot_general(q_heads[hd], kb, (((1,), (1,)), ((), ())),
                                preferred_element_type=_F32)
            s = s - decay[hd:hd + 1, :]
            if diagonal:
                s = jnp.where(col <= row, s, NEG_INF)
            m_old = m_scr[hd][:, 0:1]
            m_new = jnp.maximum(m_old, jnp.max(s, axis=1, keepdims=True))
            alpha = jnp.exp(m_old - m_new)
            p = jnp.exp(s - m_new)
            l_new = alpha * l_scr[hd][:, 0:1] + jnp.sum(p, axis=1, keepdims=True)
            m_scr[hd] = jnp.broadcast_to(m_new, (blk, LANES))
            l_scr[hd] = jnp.broadcast_to(l_new, (blk, LANES))
            alphas.append(alpha)
            pvs.append(jnp.dot(p.astype(_BF16), vb, preferred_element_type=_F32))
        acc_scr[...] = (jnp.where(low, alphas[0], alphas[1]) * acc_scr[...]
                        + jnp.where(low, pvs[0], pvs[1]))

    def body(j, carry):
        step(j, False)
        return carry

    lax.fori_loop(0, i, body, 0)
    step(i, True)
    denom = jnp.where(low, l_scr[0], l_scr[1])
    o_ref[...] = (acc_scr[...] / denom).astype(o_ref.dtype)


def _forgetting_attention(qkv, cum):
    bsz, seq, width = qkv.shape
    n_pairs = FOX_HEADS // 2
    n_blk = seq // FOX_BLOCK
    cum_blocks = cum.reshape(bsz, n_pairs, 2, n_blk, FOX_BLOCK).transpose(0, 1, 3, 2, 4)
    q_spec = pl.BlockSpec((None, FOX_BLOCK, LANES), lambda b, p, i: (b, i, p))
    k_spec = pl.BlockSpec((None, seq, LANES), lambda b, p, i: (b, 0, n_pairs + p))
    v_spec = pl.BlockSpec((None, seq, LANES), lambda b, p, i: (b, 0, 2 * n_pairs + p))
    cum_spec = pl.BlockSpec((None, None, n_blk, 2, FOX_BLOCK), lambda b, p, i: (b, p, 0, 0, 0))
    return pl.pallas_call(
        functools.partial(_fox_kernel, blk=FOX_BLOCK),
        out_shape=jax.ShapeDtypeStruct((bsz, seq, FOX_WIDTH), _BF16),
        grid=(bsz, n_pairs, n_blk),
        in_specs=[q_spec, k_spec, v_spec, cum_spec],
        out_specs=pl.BlockSpec((None, FOX_BLOCK, LANES), lambda b, p, i: (b, i, p)),
        scratch_shapes=[pltpu.VMEM((FOX_BLOCK, LANES), _F32),
                        pltpu.VMEM((2, FOX_BLOCK, LANES), _F32),
                        pltpu.VMEM((2, FOX_BLOCK, LANES), _F32)],
        compiler_params=_params("parallel", "parallel", "arbitrary"),
        name="forgetting_attention",
    )(qkv, qkv, qkv, cum_blocks)


def _merge_kernel(x_ref, gt_ref, o1_ref, o2_ref, o3_ref, l1_ref, l2_ref, l3_ref, yb_ref,
                  ga_ref, gb_ref, wa_ref, wb_ref, wo_ref, out_ref):
    l1, l2, l3 = l1_ref[...], l2_ref[...], l3_ref[...]
    top = jnp.maximum(jnp.maximum(l1, l2), l3)
    e1, e2, e3 = jnp.exp(l1 - top), jnp.exp(l2 - top), jnp.exp(l3 - top)
    y_a = (e1 * o1_ref[...].astype(_F32) + e2 * o2_ref[...].astype(_F32)
           + e3 * o3_ref[...].astype(_F32)) / (e1 + e2 + e3)
    br_a = jnp.dot(y_a.astype(_BF16), wa_ref[...], preferred_element_type=_F32)
    br_b = jnp.dot(yb_ref[...], wb_ref[...], preferred_element_type=_F32)
    merged = ga_ref[...].astype(_F32) * br_a + gb_ref[...].astype(_F32) * br_b
    mixed = jnp.dot(merged.astype(_BF16), wo_ref[...], preferred_element_type=_F32)
    out_ref[...] = x_ref[...] + gt_ref[...] * mixed


def _merge(x, gate, dil_outs, dil_lses, y_b, gates, w_branch_a, w_branch_b, w_out):
    bsz, seq, d = x.shape
    row = lambda n, c=0: pl.BlockSpec((None, PROJ_ROWS, n), lambda b, i: (b, i, c))
    return pl.pallas_call(
        _merge_kernel,
        out_shape=jax.ShapeDtypeStruct(x.shape, _F32),
        grid=(bsz, seq // PROJ_ROWS),
        in_specs=[row(d), pl.BlockSpec((None, 1, d), lambda b, i: (b, 0, 0)),
                  row(DIL_OUT), row(DIL_OUT), row(DIL_OUT),
                  row(DIL_OUT), row(DIL_OUT), row(DIL_OUT),
                  row(FOX_WIDTH), row(d, 0), row(d, 1),
                  _const_spec(w_branch_a.shape), _const_spec(w_branch_b.shape),
                  _const_spec(w_out.shape)],
        out_specs=row(d),
        compiler_params=_params("parallel", "parallel"),
        name="branch_merge",
    )(x, gate, *dil_outs, *dil_lses, y_b, gates, gates, w_branch_a, w_branch_b, w_out)


def kernel(x, c, ada_w, ada_b, norm_ffn1, ffn1_w_gate, ffn1_w_up, ffn1_w_down, norm_mix, w_in,
           forget_bias, w_branch_a, w_branch_b, w_out, norm_ffn2, ffn2_w_gate, ffn2_w_up,
           ffn2_w_down, norm_final):
    bsz, seq, d = x.shape
    depth = ada_w.shape[0]
    inv_sqrt_e = 1.0 / math.sqrt(HEAD_DIM)
    offsets = [int(o) for o in np.cumsum(IN_SIZES)[:-1]]
    for layer in range(depth):
        mod = _ada_modulation(c, ada_w[layer], ada_b[layer])
        sh1, sc1, gt1, sh2, sc2, gt2, sh3, sc3, gt3 = (
            m.reshape(bsz, 1, d) for m in jnp.split(mod, N_MOD, axis=-1))

        x = _ffn(x, sh1, sc1, gt1, norm_ffn1[layer], ffn1_w_gate[layer], ffn1_w_up[layer],
                 ffn1_w_down[layer])

        wqa, wka, wva, wqb, wkb, wvb, wf, wga, wgb = jnp.split(w_in[layer], offsets, axis=-1)
        w_a = jnp.concatenate([wqa * inv_sqrt_e, wka, wva], axis=-1).astype(_BF16)
        w_b = jnp.concatenate([wqb * inv_sqrt_e, wkb, wvb], axis=-1).astype(_BF16)
        w_gates = jnp.concatenate([wga, wgb], axis=-1).astype(_BF16)
        qkv_a, qkv_b, gates, cum = _mixer_projection(
            x, sh2, sc2, norm_mix[layer], w_a, w_b, w_gates, wf.T.astype(_BF16),
            forget_bias[layer].reshape(FOX_HEADS, 1))

        dil = [_dilated_attention(qkv_a, g) for g in range(N_DIL)]
        y_b = _forgetting_attention(qkv_b, cum)
        x = _merge(x, gt2, [o for o, _ in dil], [l for _, l in dil], y_b, gates,
                   w_branch_a[layer].astype(_BF16), w_branch_b[layer].astype(_BF16),
                   w_out[layer].astype(_BF16))

        last = layer == depth - 1
        x = _ffn(x, sh3, sc3, gt3, norm_ffn2[layer], ffn2_w_gate[layer], ffn2_w_up[layer],
                 ffn2_w_down[layer], final_gain=norm_final if last else None)
    if depth == 0:
        raise ValueError("depth must be positive")
    return x
```

```python
import functools
import math

import numpy as np
import jax
import jax.numpy as jnp
from jax import lax
from jax.experimental import pallas as pl
from jax.experimental.pallas import tpu as pltpu

D_MODEL = 1024
HEAD_DIM = 64
DIL_GROUPS = ((128, 1), (512, 4), (2048, 16))
N_DIL = len(DIL_GROUPS)
DIL_HEADS_PER_GROUP = 4
DIL_HEADS = N_DIL * DIL_HEADS_PER_GROUP
DIL_WIDTH = DIL_HEADS * HEAD_DIM
DIL_OUT = DIL_HEADS_PER_GROUP * HEAD_DIM
FOX_HEADS = 8
FOX_PAIRS = FOX_HEADS // 2
FOX_WIDTH = FOX_HEADS * HEAD_DIM
BLOCK = 128
N_MOD = 9
RMS_EPS = 1e-6
ALIBI_MAX_BIAS = 8.0
NEG_INF = -1e30
IN_SIZES = (DIL_WIDTH, DIL_WIDTH, DIL_WIDTH, FOX_WIDTH, FOX_WIDTH, FOX_WIDTH,
            FOX_HEADS, D_MODEL, D_MODEL)

LANES = 128
SUBLANES = 8
VMEM_LIMIT_BYTES = 56 * 1024 * 1024

FFN_ROWS = 512
FFN_CHUNK = 256
PROJ_ROWS = 512
PROJ_CHUNK = 256
FOX_Q = 512
FOX_K = 256
DECAY_PIECES = 3
ADA_COLS = 1024

_F32 = jnp.float32
_BF16 = jnp.bfloat16


def _const_spec(shape):
    zeros = (0,) * len(shape)
    return pl.BlockSpec(shape, lambda *_: zeros, pipeline_mode=pl.Buffered(1))


def _params(*semantics):
    return pltpu.CompilerParams(dimension_semantics=semantics,
                                vmem_limit_bytes=VMEM_LIMIT_BYTES)


def _modulated_norm(x, gain, shift, scale):
    ms = jnp.mean(x * x, axis=-1, keepdims=True)
    y = x * lax.rsqrt(ms + RMS_EPS) * gain
    return y * (1.0 + scale) + shift


def _dot_nt(a, b):
    return lax.dot_general(a, b, (((1,), (1,)), ((), ())), preferred_element_type=_F32)


def _ada_kernel(c_ref, w_ref, b_ref, o_ref):
    c = c_ref[...]
    act = c * jax.nn.sigmoid(c)
    o_ref[...] = jnp.dot(act, w_ref[...], preferred_element_type=_F32,
                         precision=lax.Precision.HIGHEST) + b_ref[...]


def _ada_modulation(c, w, b):
    bsz, d = c.shape
    n = w.shape[1]
    return pl.pallas_call(
        _ada_kernel,
        out_shape=jax.ShapeDtypeStruct((bsz, n), _F32),
        grid=(n // ADA_COLS,),
        in_specs=[pl.BlockSpec((bsz, d), lambda j: (0, 0)),
                  pl.BlockSpec((d, ADA_COLS), lambda j: (0, j)),
                  pl.BlockSpec((1, ADA_COLS), lambda j: (0, j))],
        out_specs=pl.BlockSpec((bsz, ADA_COLS), lambda j: (0, j)),
        compiler_params=_params("arbitrary"),
        name="ada_modulation",
    )(c, w, b.reshape(1, n))


def _ffn_kernel(*refs, n_chunks, final_norm):
    if final_norm:
        (x_ref, sh_ref, sc_ref, gt_ref, g_ref, wg_ref, wu_ref, wd_ref, gf_ref,
         o_ref, h_scr, acc_scr) = refs
    else:
        (x_ref, sh_ref, sc_ref, gt_ref, g_ref, wg_ref, wu_ref, wd_ref,
         o_ref, h_scr, acc_scr) = refs
    h = _modulated_norm(x_ref[...], g_ref[...], sh_ref[...], sc_ref[...])
    h_scr[...] = h.astype(_BF16)
    acc_scr[...] = jnp.zeros_like(acc_scr)

    def body(j, carry):
        hb = h_scr[...]
        g = jnp.dot(hb, wg_ref[j], preferred_element_type=_F32)
        u = jnp.dot(hb, wu_ref[j], preferred_element_type=_F32)
        a = (g * jax.nn.sigmoid(g) * u).astype(_BF16)
        acc_scr[...] += jnp.dot(a, wd_ref[j], preferred_element_type=_F32)
        return carry

    lax.fori_loop(0, n_chunks, body, 0)
    out = x_ref[...] + 0.5 * gt_ref[...] * acc_scr[...]
    if final_norm:
        ms = jnp.mean(out * out, axis=-1, keepdims=True)
        out = out * lax.rsqrt(ms + RMS_EPS) * gf_ref[...]
    o_ref[...] = out


def _ffn(x, shift, scale, gate, gain, w_gate, w_up, w_down, final_gain=None):
    bsz, seq, d = x.shape
    d_ff = w_gate.shape[1]
    n_chunks = d_ff // FFN_CHUNK
    wg = w_gate.astype(_BF16).reshape(d, n_chunks, FFN_CHUNK).transpose(1, 0, 2)
    wu = w_up.astype(_BF16).reshape(d, n_chunks, FFN_CHUNK).transpose(1, 0, 2)
    wd = w_down.astype(_BF16).reshape(n_chunks, FFN_CHUNK, d)
    row_spec = pl.BlockSpec((None, FFN_ROWS, d), lambda b, i: (b, i, 0))
    mod_spec = pl.BlockSpec((None, 1, d), lambda b, i: (b, 0, 0))
    in_specs = [row_spec, mod_spec, mod_spec, mod_spec, _const_spec((1, d)),
                _const_spec(wg.shape), _const_spec(wu.shape), _const_spec(wd.shape)]
    args = [x, shift, scale, gate, gain.reshape(1, d), wg, wu, wd]
    if final_gain is not None:
        in_specs.append(_const_spec((1, d)))
        args.append(final_gain.reshape(1, d))
    return pl.pallas_call(
        functools.partial(_ffn_kernel, n_chunks=n_chunks, final_norm=final_gain is not None),
        out_shape=jax.ShapeDtypeStruct(x.shape, _F32),
        grid=(bsz, seq // FFN_ROWS),
        in_specs=in_specs,
        out_specs=row_spec,
        scratch_shapes=[pltpu.VMEM((FFN_ROWS, d), _BF16), pltpu.VMEM((FFN_ROWS, d), _F32)],
        compiler_params=_params("parallel", "parallel"),
        name="ffn_final" if final_gain is not None else "ffn",
    )(*args)


def _row_cumsum(x):
    n = x.shape[0]
    row = lax.broadcasted_iota(jnp.int32, x.shape, 0)
    shift = 1
    while shift < n:
        x = x + jnp.where(row >= shift, pltpu.roll(x, shift, 0), 0.0)
        shift *= 2
    return x


def _proj_kernel(x_ref, sh_ref, sc_ref, g_ref, wd1_ref, wd4_ref, wd16_ref, wq_ref, wk_ref, wv_ref,
                 wgate_ref, wf_ref, fb_ref,
                 od1_ref, od4_ref, od16_ref, oq_ref, ok_ref, ov_ref, okx_ref, og_ref,
                 hf_scr, h_scr, hp4_scr, hp16_scr, carry_scr):
    rows = x_ref.shape[0]
    h = _modulated_norm(x_ref[...], g_ref[...], sh_ref[...], sc_ref[...])
    h_scr[...] = h.astype(_BF16)
    for cb in range(hf_scr.shape[0]):
        hf_scr[cb] = h[:, cb * LANES:(cb + 1) * LANES]
    for dilation, hp_scr in ((4, hp4_scr), (16, hp16_scr)):
        n = rows // dilation
        for r in range(dilation):
            for cb in range(hf_scr.shape[0]):
                hp_scr[r * n:(r + 1) * n, cb * LANES:(cb + 1) * LANES] = (
                    hf_scr[cb, pl.ds(r, n, stride=dilation), :].astype(_BF16))

    for dilation, hp_scr, w_ref, o_ref in ((1, h_scr, wd1_ref, od1_ref), (4, hp4_scr, wd4_ref, od4_ref),
                                           (16, hp16_scr, wd16_ref, od16_ref)):
        n = rows // dilation
        for c in range(w_ref.shape[1] // PROJ_CHUNK):
            cols = slice(c * PROJ_CHUNK, (c + 1) * PROJ_CHUNK)
            y = jnp.dot(hp_scr[...], w_ref[:, cols], preferred_element_type=_F32).astype(_BF16)
            for r in range(dilation):
                o_ref[r, :, cols] = y[r * n:(r + 1) * n, :]

    for c in range(FOX_WIDTH // PROJ_CHUNK):
        feat = slice(c * PROJ_CHUNK, (c + 1) * PROJ_CHUNK)
        oq_ref[feat, :] = _dot_nt(wq_ref[feat, :], h_scr[...]).astype(_BF16)
        v_t = _dot_nt(wv_ref[feat, :], h_scr[...]).astype(_BF16)
        for kb in range(rows // FOX_K):
            ov_ref[kb, feat, :] = v_t[:, kb * FOX_K:(kb + 1) * FOX_K]
        ok_ref[:, feat] = jnp.dot(h_scr[...], wk_ref[:, feat],
                                  preferred_element_type=_F32).astype(_BF16)

    for c in range(wgate_ref.shape[1] // PROJ_CHUNK):
        cols = slice(c * PROJ_CHUNK, (c + 1) * PROJ_CHUNK)
        y = jnp.dot(h_scr[...], wgate_ref[:, cols], preferred_element_type=_F32)
        og_ref[:, cols] = jax.nn.sigmoid(y).astype(_BF16)

    @pl.when(pl.program_id(1) == 0)
    def _():
        carry_scr[...] = jnp.zeros_like(carry_scr)

    f = jnp.dot(h_scr[...], wf_ref[...], preferred_element_type=_F32) + fb_ref[...]
    log_sig = jnp.minimum(f, 0.0) - jnp.log1p(jnp.exp(-jnp.abs(f)))
    cum = _row_cumsum(log_sig) + carry_scr[0:1, :]
    carry_scr[...] = jnp.broadcast_to(cum[rows - 1:rows, :], carry_scr.shape)
    p1 = cum.astype(_BF16)
    r1 = cum - p1.astype(_F32)
    p2 = r1.astype(_BF16)
    p3 = (r1 - p2.astype(_F32)).astype(_BF16)
    lane = lax.broadcasted_iota(jnp.int32, cum.shape, 1)
    piece = lane % DECAY_PIECES
    pieces = jnp.where(piece == 0, p1.astype(_F32),
                       jnp.where(piece == 1, p2.astype(_F32), p3.astype(_F32)))
    okx_ref[...] = jnp.where(lane < DECAY_PIECES * FOX_HEADS, pieces, 0.0).astype(_BF16)


def _mixer_projection(x, shift, scale, gain, w_dil, w_q_t, w_k, w_v_t, w_gates, w_f, f_bias):
    bsz, seq, d = x.shape
    rows = PROJ_ROWS
    n_tiles = seq // rows
    tile = lambda b, i: (b, i, 0)
    row = lambda n: pl.BlockSpec((None, rows, n), tile)
    mod_spec = pl.BlockSpec((None, 1, d), lambda b, i: (b, 0, 0))
    dil_shapes, dil_specs = [], []
    for _, dilation in DIL_GROUPS:
        dil_shapes.append(jax.ShapeDtypeStruct((bsz, dilation, seq // dilation, 3 * DIL_OUT), _BF16))
        dil_specs.append(pl.BlockSpec((None, dilation, rows // dilation, 3 * DIL_OUT),
                                      lambda b, i: (b, 0, i, 0)))
    out_shape = (*dil_shapes,
                 jax.ShapeDtypeStruct((bsz, n_tiles, FOX_WIDTH, rows), _BF16),
                 jax.ShapeDtypeStruct((bsz, seq, FOX_WIDTH), _BF16),
                 jax.ShapeDtypeStruct((bsz, seq // FOX_K, FOX_WIDTH, FOX_K), _BF16),
                 jax.ShapeDtypeStruct((bsz, seq, LANES), _BF16),
                 jax.ShapeDtypeStruct((bsz, seq, w_gates.shape[1]), _BF16))
    out_specs = (*dil_specs,
                 pl.BlockSpec((None, None, FOX_WIDTH, rows), lambda b, i: (b, i, 0, 0)),
                 row(FOX_WIDTH),
                 pl.BlockSpec((None, rows // FOX_K, FOX_WIDTH, FOX_K), lambda b, i: (b, i, 0, 0)),
                 row(LANES),
                 row(w_gates.shape[1]))
    weights = [*w_dil, w_q_t, w_k, w_v_t, w_gates, w_f, f_bias]
    return pl.pallas_call(
        _proj_kernel,
        out_shape=out_shape,
        grid=(bsz, n_tiles),
        in_specs=[row(d), mod_spec, mod_spec, _const_spec((1, d)),
                  *[_const_spec(w.shape) for w in weights]],
        out_specs=out_specs,
        scratch_shapes=[pltpu.VMEM((d // LANES, rows, LANES), _F32), pltpu.VMEM((rows, d), _BF16),
                        pltpu.VMEM((rows, d), _BF16), pltpu.VMEM((rows, d), _BF16),
                        pltpu.VMEM((SUBLANES, LANES), _F32)],
        compiler_params=_params("parallel", "arbitrary"),
        name="mixer_projection",
    )(x, shift, scale, gain.reshape(1, d), *weights)


def _dil_kernel(q_ref, kp_ref, kc_ref, vp_ref, vc_ref, bias_ref, o_ref, lse_ref,
                k_scr, v_scr, *, rows):
    k_scr[0:BLOCK, :] = kp_ref[...]
    k_scr[BLOCK:, :] = kc_ref[...]
    v_scr[0:BLOCK, :] = vp_ref[...]
    v_scr[BLOCK:, :] = vc_ref[...]
    first_tile = pl.program_id(2) == 0
    heads = range(DIL_HEADS_PER_GROUP)
    chunks = range(rows // BLOCK)
    lane = lax.broadcasted_iota(jnp.int32, (BLOCK, DIL_OUT), 1)
    head_mask = [(lane >= h * HEAD_DIM) & (lane < (h + 1) * HEAD_DIM) for h in heads]
    key = lax.broadcasted_iota(jnp.int32, (len(heads) * BLOCK, 2 * BLOCK), 1)

    scores = []
    for c in chunks:
        q = q_ref[c * BLOCK:(c + 1) * BLOCK, :]
        q4 = jnp.concatenate([jnp.where(head_mask[h], q, jnp.zeros_like(q)) for h in heads], axis=0)
        s = _dot_nt(q4, k_scr[c * BLOCK:(c + 2) * BLOCK, :]) + bias_ref[...]
        if c == 0:
            s = jnp.where(first_tile & (key < BLOCK), NEG_INF, s)
        scores.append(s)
    tops = [jnp.max(s, axis=1, keepdims=True) for s in scores]
    probs = [jnp.exp(s - m) for s, m in zip(scores, tops)]
    sums = [jnp.sum(p, axis=1, keepdims=True) for p in probs]
    outs = [jnp.dot(p.astype(_BF16), v_scr[c * BLOCK:(c + 2) * BLOCK, :], preferred_element_type=_F32)
            for c, p in zip(chunks, probs)]
    for c in chunks:
        normed = outs[c] / sums[c]
        lse4 = tops[c] + jnp.log(sums[c])
        o = jnp.zeros((BLOCK, DIL_OUT), _F32)
        lse = jnp.zeros((BLOCK, DIL_OUT), _F32)
        for h in heads:
            o = jnp.where(head_mask[h], normed[h * BLOCK:(h + 1) * BLOCK, :], o)
            lse = jnp.where(head_mask[h], lse4[h * BLOCK:(h + 1) * BLOCK, :], lse)
        o_ref[c * BLOCK:(c + 1) * BLOCK, :] = o.astype(o_ref.dtype)
        lse_ref[c * BLOCK:(c + 1) * BLOCK, :] = lse


def _dilated_bias(group):
    window, dilation = DIL_GROUPS[group]
    n_back = window // dilation
    slopes = np.array([2.0 ** (-ALIBI_MAX_BIAS * (i + 1) / DIL_HEADS) for i in range(DIL_HEADS)],
                      dtype=np.float32).reshape(N_DIL, DIL_HEADS_PER_GROUP)[group]
    qi = np.arange(BLOCK)[:, None]
    ki = np.arange(2 * BLOCK)[None, :]
    rel = BLOCK + qi - ki
    valid = (rel >= 0) & (rel <= n_back)
    bias = -slopes[:, None, None] * (rel * dilation).astype(np.float32)[None]
    bias = np.where(valid[None], bias, np.float32(NEG_INF)).astype(np.float32)
    return jnp.asarray(bias.reshape(DIL_HEADS_PER_GROUP * BLOCK, 2 * BLOCK))


def _dilated_attention(qkv, group):
    bsz, dilation, sub_len, _ = qkv.shape
    rows = min(sub_len, 4 * BLOCK)
    n_tiles = sub_len // rows
    blocks_per_tile = rows // BLOCK

    def cur(n):
        return pl.BlockSpec((None, None, rows, DIL_OUT), lambda b, r, i: (b, r, i, n))

    def prev(n):
        return pl.BlockSpec((None, None, BLOCK, DIL_OUT),
                            lambda b, r, i: (b, r, jnp.maximum(i * blocks_per_tile - 1, 0), n))

    out_spec = pl.BlockSpec((None, None, rows, DIL_OUT), lambda b, r, i: (b, r, i, 0))
    return pl.pallas_call(
        functools.partial(_dil_kernel, rows=rows),
        out_shape=(jax.ShapeDtypeStruct((bsz, dilation, sub_len, DIL_OUT), _BF16),
                   jax.ShapeDtypeStruct((bsz, dilation, sub_len, DIL_OUT), _F32)),
        grid=(bsz, dilation, n_tiles),
        in_specs=[cur(0), prev(1), cur(1), prev(2), cur(2),
                  _const_spec((DIL_HEADS_PER_GROUP * BLOCK, 2 * BLOCK))],
        out_specs=(out_spec, out_spec),
        scratch_shapes=[pltpu.VMEM((rows + BLOCK, DIL_OUT), _BF16),
                        pltpu.VMEM((rows + BLOCK, DIL_OUT), _BF16)],
        compiler_params=_params("parallel", "parallel", "parallel"),
        name=f"dilated_attention_d{dilation}",
    )(qkv, qkv, qkv, qkv, qkv, _dilated_bias(group))


def _fox_kernel(q_ref, k_ref, kx_ref, v_ref, o_ref, qa_scr, acc_scr, m_scr, l_scr):
    pair = pl.program_id(1)
    i = pl.program_id(2)
    nq = q_ref.shape[1]
    half = nq // 2
    tap = lax.broadcasted_iota(jnp.int32, (LANES, nq), 0)
    zeros = jnp.zeros((HEAD_DIM, nq), _BF16)
    for hd in range(2):
        qa_scr[hd, 0:HEAD_DIM, :] = q_ref[0:HEAD_DIM, :] if hd == 0 else zeros
        qa_scr[hd, HEAD_DIM:LANES, :] = zeros if hd == 0 else q_ref[HEAD_DIM:LANES, :]
        first = DECAY_PIECES * (2 * pair + hd)
        qa_scr[hd, LANES:, :] = jnp.where((tap >= first) & (tap < first + DECAY_PIECES),
                                          -1.0, 0.0).astype(_BF16)
    acc_scr[...] = jnp.zeros_like(acc_scr)
    m_scr[...] = jnp.full_like(m_scr, NEG_INF)
    l_scr[...] = jnp.zeros_like(l_scr)

    def step(j, lo, masked):
        cols = slice(lo, nq)
        start = pl.multiple_of(j * FOX_K, FOX_K)
        k_aug = jnp.concatenate([k_ref[pl.ds(start, FOX_K), :], kx_ref[pl.ds(start, FOX_K), :]],
                                axis=1)
        v_t = v_ref[j]
        scores = [jnp.dot(k_aug, qa_scr[hd, :, cols], preferred_element_type=_F32)
                  for hd in range(2)]
        if masked:
            key_pos = j * FOX_K + lax.broadcasted_iota(jnp.int32, scores[0].shape, 0)
            q_pos = i * nq + lo + lax.broadcasted_iota(jnp.int32, scores[0].shape, 1)
            scores = [jnp.where(key_pos <= q_pos, s, NEG_INF) for s in scores]
        m_old = [m_scr[hd, 0:1, cols] for hd in range(2)]
        m_new = [jnp.maximum(m, jnp.max(s, axis=0, keepdims=True)) for m, s in zip(m_old, scores)]
        probs = [jnp.exp(s - m) for s, m in zip(scores, m_new)]
        alpha = [jnp.exp(mo - mn) for mo, mn in zip(m_old, m_new)]
        for hd in range(2):
            rows = slice(hd * HEAD_DIM, (hd + 1) * HEAD_DIM)
            l_new = alpha[hd] * l_scr[hd, 0:1, cols] + jnp.sum(probs[hd], axis=0, keepdims=True)
            pv = jnp.dot(v_t[rows, :], probs[hd].astype(_BF16), preferred_element_type=_F32)
            acc_scr[rows, cols] = alpha[hd] * acc_scr[rows, cols] + pv
            m_scr[hd, :, cols] = jnp.broadcast_to(m_new[hd], (SUBLANES, nq - lo))
            l_scr[hd, :, cols] = jnp.broadcast_to(l_new, (SUBLANES, nq - lo))

    def body(j, carry):
        step(j, 0, False)
        return carry

    blocks_per_step = nq // FOX_K
    lax.fori_loop(0, blocks_per_step * i, body, 0)
    step(blocks_per_step * i, 0, True)
    step(blocks_per_step * i + 1, half, True)
    denom = jnp.concatenate([jnp.broadcast_to(l_scr[hd, 0:1, :], (HEAD_DIM, nq)) for hd in range(2)],
                            axis=0)
    o_ref[...] = (acc_scr[...] / denom).T.astype(o_ref.dtype)


def _forgetting_attention(q_t, k, k_decay, v_t):
    bsz, seq, _ = k.shape
    n_q = q_t.shape[1]
    assert q_t.shape[3] == FOX_Q == 2 * FOX_K and v_t.shape[3] == FOX_K
    return pl.pallas_call(
        _fox_kernel,
        out_shape=jax.ShapeDtypeStruct((bsz, seq, FOX_WIDTH), _BF16),
        grid=(bsz, FOX_PAIRS, n_q),
        in_specs=[pl.BlockSpec((None, None, LANES, FOX_Q), lambda b, p, i: (b, i, p, 0)),
                  pl.BlockSpec((None, seq, LANES), lambda b, p, i: (b, 0, p)),
                  pl.BlockSpec((None, seq, LANES), lambda b, p, i: (b, 0, 0)),
                  pl.BlockSpec((None, seq // FOX_K, LANES, FOX_K), lambda b, p, i: (b, 0, p, 0))],
        out_specs=pl.BlockSpec((None, FOX_Q, LANES), lambda b, p, i: (b, i, p)),
        scratch_shapes=[pltpu.VMEM((2, 2 * LANES, FOX_Q), _BF16),
                        pltpu.VMEM((LANES, FOX_Q), _F32),
                        pltpu.VMEM((2, SUBLANES, FOX_Q), _F32),
                        pltpu.VMEM((2, SUBLANES, FOX_Q), _F32)],
        compiler_params=_params("parallel", "parallel", "arbitrary"),
        name="forgetting_attention",
    )(q_t, k, k_decay, v_t)


def _merge_kernel(x_ref, gt_ref, o1_ref, o4_ref, o16_ref, l1_ref, l4_ref, l16_ref, yb_ref,
                  ga_ref, gb_ref, wa_ref, wb_ref, wo_ref, out_ref, o_scr, l_scr):
    rows = x_ref.shape[0]
    for g, (dilation, o_ref, l_ref) in enumerate(((4, o4_ref, l4_ref), (16, o16_ref, l16_ref))):
        n = rows // dilation
        for r in range(dilation):
            o_res = o_ref[r].astype(_F32)
            l_res = l_ref[r]
            for cb in range(DIL_OUT // LANES):
                lanes = slice(cb * LANES, (cb + 1) * LANES)
                o_scr[g, cb, pl.ds(r, n, stride=dilation), :] = o_res[:, lanes]
                l_scr[g, cb, pl.ds(r, n, stride=dilation), :] = l_res[:, lanes]

    def token_order(scr, g):
        return jnp.concatenate([scr[g, cb] for cb in range(DIL_OUT // LANES)], axis=1)

    l1, l2, l3 = l1_ref[0], token_order(l_scr, 0), token_order(l_scr, 1)
    top = jnp.maximum(jnp.maximum(l1, l2), l3)
    e1, e2, e3 = jnp.exp(l1 - top), jnp.exp(l2 - top), jnp.exp(l3 - top)
    y_a = (e1 * o1_ref[0].astype(_F32) + e2 * token_order(o_scr, 0)
           + e3 * token_order(o_scr, 1)) / (e1 + e2 + e3)
    br_a = jnp.dot(y_a.astype(_BF16), wa_ref[...], preferred_element_type=_F32)
    br_b = jnp.dot(yb_ref[...], wb_ref[...], preferred_element_type=_F32)
    merged = ga_ref[...].astype(_F32) * br_a + gb_ref[...].astype(_F32) * br_b
    mixed = jnp.dot(merged.astype(_BF16), wo_ref[...], preferred_element_type=_F32)
    out_ref[...] = x_ref[...] + gt_ref[...] * mixed


def _merge(x, gate, dil_outs, dil_lses, y_b, gates, w_branch_a, w_branch_b, w_out):
    bsz, seq, d = x.shape
    rows = PROJ_ROWS
    row = lambda n, c=0: pl.BlockSpec((None, rows, n), lambda b, i: (b, i, c))
    dil_specs = [pl.BlockSpec((None, dilation, rows // dilation, DIL_OUT), lambda b, i: (b, 0, i, 0))
                 for _, dilation in DIL_GROUPS]
    return pl.pallas_call(
        _merge_kernel,
        out_shape=jax.ShapeDtypeStruct(x.shape, _F32),
        grid=(bsz, seq // rows),
        in_specs=[row(d), pl.BlockSpec((None, 1, d), lambda b, i: (b, 0, 0)),
                  *dil_specs, *dil_specs,
                  row(FOX_WIDTH), row(d, 0), row(d, 1),
                  _const_spec(w_branch_a.shape), _const_spec(w_branch_b.shape),
                  _const_spec(w_out.shape)],
        out_specs=row(d),
        scratch_shapes=[pltpu.VMEM((N_DIL - 1, DIL_OUT // LANES, rows, LANES), _F32),
                        pltpu.VMEM((N_DIL - 1, DIL_OUT // LANES, rows, LANES), _F32)],
        compiler_params=_params("parallel", "parallel"),
        name="branch_merge",
    )(x, gate, *dil_outs, *dil_lses, y_b, gates, gates, w_branch_a, w_branch_b, w_out)


def _decay_columns(values):
    rep = jnp.repeat(values, DECAY_PIECES, axis=-1)
    pad = [(0, 0)] * (values.ndim - 1) + [(0, LANES - rep.shape[-1])]
    return jnp.pad(rep, pad)


def kernel(x, c, ada_w, ada_b, norm_ffn1, ffn1_w_gate, ffn1_w_up, ffn1_w_down, norm_mix, w_in,
           forget_bias, w_branch_a, w_branch_b, w_out, norm_ffn2, ffn2_w_gate, ffn2_w_up,
           ffn2_w_down, norm_final):
    bsz, seq, d = x.shape
    depth = ada_w.shape[0]
    if depth < 1:
        raise ValueError("depth must be positive")
    inv_sqrt_e = 1.0 / math.sqrt(HEAD_DIM)
    offsets = [int(o) for o in np.cumsum(IN_SIZES)[:-1]]
    for layer in range(depth):
        mod = _ada_modulation(c, ada_w[layer], ada_b[layer])
        sh1, sc1, gt1, sh2, sc2, gt2, sh3, sc3, gt3 = (
            m.reshape(bsz, 1, d) for m in jnp.split(mod, N_MOD, axis=-1))

        x = _ffn(x, sh1, sc1, gt1, norm_ffn1[layer], ffn1_w_gate[layer], ffn1_w_up[layer],
                 ffn1_w_down[layer])

        wqa, wka, wva, wqb, wkb, wvb, wf, wga, wgb = jnp.split(w_in[layer], offsets, axis=-1)
        w_dil = []
        for g in range(N_DIL):
            cols = slice(g * DIL_OUT, (g + 1) * DIL_OUT)
            w_dil.append(jnp.concatenate([wqa[:, cols] * inv_sqrt_e, wka[:, cols], wva[:, cols]],
                                         axis=-1).astype(_BF16))
        w_gates = jnp.concatenate([wga, wgb], axis=-1).astype(_BF16)
        qkv1, qkv4, qkv16, q_t, k_b, v_t, k_decay, gates = _mixer_projection(
            x, sh2, sc2, norm_mix[layer], w_dil, (wqb * inv_sqrt_e).T.astype(_BF16),
            wkb.astype(_BF16), wvb.T.astype(_BF16), w_gates,
            _decay_columns(wf).astype(_BF16), _decay_columns(forget_bias[layer][None, :]))

        dil = [_dilated_attention(qkv, g) for g, qkv in enumerate((qkv1, qkv4, qkv16))]
        y_b = _forgetting_attention(q_t, k_b, k_decay, v_t)
        x = _merge(x, gt2, [o for o, _ in dil], [l for _, l in dil], y_b, gates,
                   w_branch_a[layer].astype(_BF16), w_branch_b[layer].astype(_BF16),
                   w_out[layer].astype(_BF16))

        last = layer == depth - 1
        x = _ffn(x, sh3, sc3, gt3, norm_ffn2[layer], ffn2_w_gate[layer], ffn2_w_up[layer],
                 ffn2_w_down[layer], final_gain=norm_final if last else None)
    return x
```

```python
import functools
import math

import numpy as np
import jax
import jax.numpy as jnp
from jax import lax
from jax.experimental import pallas as pl
from jax.experimental.pallas import tpu as pltpu

D_MODEL = 1024
HEAD_DIM = 64
DIL_GROUPS = ((128, 1), (512, 4), (2048, 16))
N_DIL = len(DIL_GROUPS)
DIL_HEADS_PER_GROUP = 4
DIL_HEADS = N_DIL * DIL_HEADS_PER_GROUP
DIL_WIDTH = DIL_HEADS * HEAD_DIM
DIL_OUT = DIL_HEADS_PER_GROUP * HEAD_DIM
FOX_HEADS = 8
FOX_PAIRS = FOX_HEADS // 2
FOX_WIDTH = FOX_HEADS * HEAD_DIM
BLOCK = 128
N_MOD = 9
RMS_EPS = 1e-6
ALIBI_MAX_BIAS = 8.0
NEG_INF = -1e30
IN_SIZES = (DIL_WIDTH, DIL_WIDTH, DIL_WIDTH, FOX_WIDTH, FOX_WIDTH, FOX_WIDTH,
            FOX_HEADS, D_MODEL, D_MODEL)

LANES = 128
SUBLANES = 8
VMEM_LIMIT_BYTES = 56 * 1024 * 1024

FFN_ROWS = 512
FFN_CHUNK = 256
PROJ_ROWS = 512
PROJ_CHUNK = 256
FOX_Q = 512
FOX_K = 256
FOX_ACC_ROWS = HEAD_DIM + 16
DECAY_PIECES = 3
LOG2_E = math.log2(math.e)
ADA_COLS = 1024

_F32 = jnp.float32
_BF16 = jnp.bfloat16


def _const_spec(shape):
    zeros = (0,) * len(shape)
    return pl.BlockSpec(shape, lambda *_: zeros, pipeline_mode=pl.Buffered(1))


def _params(*semantics):
    return pltpu.CompilerParams(dimension_semantics=semantics,
                                vmem_limit_bytes=VMEM_LIMIT_BYTES)


def _modulated_norm(x, gain, shift, scale):
    ms = jnp.mean(x * x, axis=-1, keepdims=True)
    y = x * lax.rsqrt(ms + RMS_EPS) * gain
    return y * (1.0 + scale) + shift


def _dot_nt(a, b):
    return lax.dot_general(a, b, (((1,), (1,)), ((), ())), preferred_element_type=_F32)


def _ada_kernel(c_ref, w_ref, b_ref, o_ref):
    c = c_ref[...]
    act = c * jax.nn.sigmoid(c)
    o_ref[...] = jnp.dot(act, w_ref[...], preferred_element_type=_F32,
                         precision=lax.Precision.HIGHEST) + b_ref[...]


def _ada_modulation(c, w, b):
    bsz, d = c.shape
    n = w.shape[1]
    return pl.pallas_call(
        _ada_kernel,
        out_shape=jax.ShapeDtypeStruct((bsz, n), _F32),
        grid=(n // ADA_COLS,),
        in_specs=[pl.BlockSpec((bsz, d), lambda j: (0, 0)),
                  pl.BlockSpec((d, ADA_COLS), lambda j: (0, j)),
                  pl.BlockSpec((1, ADA_COLS), lambda j: (0, j))],
        out_specs=pl.BlockSpec((bsz, ADA_COLS), lambda j: (0, j)),
        compiler_params=_params("arbitrary"),
        name="ada_modulation",
    )(c, w, b.reshape(1, n))


def _ffn_kernel(*refs, n_chunks, final_norm):
    if final_norm:
        (x_ref, sh_ref, sc_ref, gt_ref, g_ref, wg_ref, wu_ref, wd_ref, gf_ref,
         o_ref, h_scr, acc_scr) = refs
    else:
        (x_ref, sh_ref, sc_ref, gt_ref, g_ref, wg_ref, wu_ref, wd_ref,
         o_ref, h_scr, acc_scr) = refs
    h = _modulated_norm(x_ref[...], g_ref[...], sh_ref[...], sc_ref[...])
    h_scr[...] = h.astype(_BF16)
    for j in range(n_chunks):
        hb = h_scr[...]
        g = jnp.dot(hb, wg_ref[j], preferred_element_type=_F32)
        u = jnp.dot(hb, wu_ref[j], preferred_element_type=_F32)
        acc_scr[:, j * FFN_CHUNK:(j + 1) * FFN_CHUNK] = (g * jax.nn.sigmoid(g) * u).astype(_BF16)
    down = jnp.dot(acc_scr[...], wd_ref[...], preferred_element_type=_F32)
    out = x_ref[...] + 0.5 * gt_ref[...] * down
    if final_norm:
        ms = jnp.mean(out * out, axis=-1, keepdims=True)
        out = out * lax.rsqrt(ms + RMS_EPS) * gf_ref[...]
    o_ref[...] = out


def _ffn(x, shift, scale, gate, gain, w_gate, w_up, w_down, final_gain=None):
    bsz, seq, d = x.shape
    d_ff = w_gate.shape[1]
    n_chunks = d_ff // FFN_CHUNK
    wg = w_gate.astype(_BF16).reshape(d, n_chunks, FFN_CHUNK).transpose(1, 0, 2)
    wu = w_up.astype(_BF16).reshape(d, n_chunks, FFN_CHUNK).transpose(1, 0, 2)
    wd = w_down.astype(_BF16)
    row_spec = pl.BlockSpec((None, FFN_ROWS, d), lambda b, i: (b, i, 0))
    mod_spec = pl.BlockSpec((None, 1, d), lambda b, i: (b, 0, 0))
    in_specs = [row_spec, mod_spec, mod_spec, mod_spec, _const_spec((1, d)),
                _const_spec(wg.shape), _const_spec(wu.shape), _const_spec(wd.shape)]
    args = [x, shift, scale, gate, gain.reshape(1, d), wg, wu, wd]
    if final_gain is not None:
        in_specs.append(_const_spec((1, d)))
        args.append(final_gain.reshape(1, d))
    return pl.pallas_call(
        functools.partial(_ffn_kernel, n_chunks=n_chunks, final_norm=final_gain is not None),
        out_shape=jax.ShapeDtypeStruct(x.shape, _F32),
        grid=(bsz, seq // FFN_ROWS),
        in_specs=in_specs,
        out_specs=row_spec,
        scratch_shapes=[pltpu.VMEM((FFN_ROWS, d), _BF16), pltpu.VMEM((FFN_ROWS, d_ff), _BF16)],
        compiler_params=_params("parallel", "parallel"),
        name="ffn_final" if final_gain is not None else "ffn",
    )(*args)


def _row_cumsum(x):
    n = x.shape[0]
    row = lax.broadcasted_iota(jnp.int32, x.shape, 0)
    shift = 1
    while shift < n:
        x = x + jnp.where(row >= shift, pltpu.roll(x, shift, 0), 0.0)
        shift *= 2
    return x


def _proj_kernel(x_ref, sh_ref, sc_ref, g_ref, wd1_ref, wd4_ref, wd16_ref, wq_ref, wk_ref, wv_ref,
                 wgate_ref, wf_ref, fb_ref,
                 od1_ref, od4_ref, od16_ref, oq_ref, ok_ref, ov_ref, okx_ref, og_ref,
                 hf_scr, h_scr, hp4_scr, hp16_scr, carry_scr):
    rows = x_ref.shape[0]
    h = _modulated_norm(x_ref[...], g_ref[...], sh_ref[...], sc_ref[...])
    h_scr[...] = h.astype(_BF16)
    for cb in range(hf_scr.shape[0]):
        hf_scr[cb] = h[:, cb * LANES:(cb + 1) * LANES]
    for dilation, hp_scr in ((4, hp4_scr), (16, hp16_scr)):
        n = rows // dilation
        for r in range(dilation):
            for cb in range(hf_scr.shape[0]):
                hp_scr[r * n:(r + 1) * n, cb * LANES:(cb + 1) * LANES] = (
                    hf_scr[cb, pl.ds(r, n, stride=dilation), :].astype(_BF16))

    for dilation, hp_scr, w_ref, o_ref in ((1, h_scr, wd1_ref, od1_ref), (4, hp4_scr, wd4_ref, od4_ref),
                                           (16, hp16_scr, wd16_ref, od16_ref)):
        n = rows // dilation
        for c in range(w_ref.shape[1] // PROJ_CHUNK):
            cols = slice(c * PROJ_CHUNK, (c + 1) * PROJ_CHUNK)
            y = jnp.dot(hp_scr[...], w_ref[:, cols], preferred_element_type=_F32).astype(_BF16)
            for r in range(dilation):
                o_ref[r, :, cols] = y[r * n:(r + 1) * n, :]

    for c in range(FOX_WIDTH // PROJ_CHUNK):
        feat = slice(c * PROJ_CHUNK, (c + 1) * PROJ_CHUNK)
        oq_ref[feat, :] = _dot_nt(wq_ref[feat, :], h_scr[...]).astype(_BF16)
        v_t = _dot_nt(wv_ref[feat, :], h_scr[...]).astype(_BF16)
        for kb in range(rows // FOX_K):
            ov_ref[kb, feat, :] = v_t[:, kb * FOX_K:(kb + 1) * FOX_K]
        ok_ref[:, feat] = jnp.dot(h_scr[...], wk_ref[:, feat],
                                  preferred_element_type=_F32).astype(_BF16)

    for c in range(wgate_ref.shape[1] // PROJ_CHUNK):
        cols = slice(c * PROJ_CHUNK, (c + 1) * PROJ_CHUNK)
        y = jnp.dot(h_scr[...], wgate_ref[:, cols], preferred_element_type=_F32)
        og_ref[:, cols] = jax.nn.sigmoid(y).astype(_BF16)

    @pl.when(pl.program_id(1) == 0)
    def _():
        carry_scr[...] = jnp.zeros_like(carry_scr)

    f = jnp.dot(h_scr[...], wf_ref[...], preferred_element_type=_F32) + fb_ref[...]
    log_sig = jnp.minimum(f, 0.0) - jnp.log1p(jnp.exp(-jnp.abs(f)))
    cum = _row_cumsum(log_sig) + carry_scr[0:1, :]
    carry_scr[...] = jnp.broadcast_to(cum[rows - 1:rows, :], carry_scr.shape)
    cum = cum * LOG2_E
    p1 = cum.astype(_BF16)
    r1 = cum - p1.astype(_F32)
    p2 = r1.astype(_BF16)
    p3 = (r1 - p2.astype(_F32)).astype(_BF16)
    lane = lax.broadcasted_iota(jnp.int32, cum.shape, 1)
    piece = lane % DECAY_PIECES
    pieces = jnp.where(piece == 0, p1.astype(_F32),
                       jnp.where(piece == 1, p2.astype(_F32), p3.astype(_F32)))
    okx_ref[...] = jnp.where(lane < DECAY_PIECES * FOX_HEADS, pieces, 0.0).astype(_BF16)


def _mixer_projection(x, shift, scale, gain, w_dil, w_q_t, w_k, w_v_t, w_gates, w_f, f_bias):
    bsz, seq, d = x.shape
    rows = PROJ_ROWS
    n_tiles = seq // rows
    tile = lambda b, i: (b, i, 0)
    row = lambda n: pl.BlockSpec((None, rows, n), tile)
    mod_spec = pl.BlockSpec((None, 1, d), lambda b, i: (b, 0, 0))
    dil_shapes, dil_specs = [], []
    for _, dilation in DIL_GROUPS:
        dil_shapes.append(jax.ShapeDtypeStruct((bsz, dilation, seq // dilation, 3 * DIL_OUT), _BF16))
        dil_specs.append(pl.BlockSpec((None, dilation, rows // dilation, 3 * DIL_OUT),
                                      lambda b, i: (b, 0, i, 0)))
    out_shape = (*dil_shapes,
                 jax.ShapeDtypeStruct((bsz, n_tiles, FOX_WIDTH, rows), _BF16),
                 jax.ShapeDtypeStruct((bsz, seq, FOX_WIDTH), _BF16),
                 jax.ShapeDtypeStruct((bsz, seq // FOX_K, FOX_WIDTH, FOX_K), _BF16),
                 jax.ShapeDtypeStruct((bsz, seq, LANES), _BF16),
                 jax.ShapeDtypeStruct((bsz, seq, w_gates.shape[1]), _BF16))
    out_specs = (*dil_specs,
                 pl.BlockSpec((None, None, FOX_WIDTH, rows), lambda b, i: (b, i, 0, 0)),
                 row(FOX_WIDTH),
                 pl.BlockSpec((None, rows // FOX_K, FOX_WIDTH, FOX_K), lambda b, i: (b, i, 0, 0)),
                 row(LANES),
                 row(w_gates.shape[1]))
    weights = [*w_dil, w_q_t, w_k, w_v_t, w_gates, w_f, f_bias]
    return pl.pallas_call(
        _proj_kernel,
        out_shape=out_shape,
        grid=(bsz, n_tiles),
        in_specs=[row(d), mod_spec, mod_spec, _const_spec((1, d)),
                  *[_const_spec(w.shape) for w in weights]],
        out_specs=out_specs,
        scratch_shapes=[pltpu.VMEM((d // LANES, rows, LANES), _F32), pltpu.VMEM((rows, d), _BF16),
                        pltpu.VMEM((rows, d), _BF16), pltpu.VMEM((rows, d), _BF16),
                        pltpu.VMEM((SUBLANES, LANES), _F32)],
        compiler_params=_params("parallel", "arbitrary"),
        name="mixer_projection",
    )(x, shift, scale, gain.reshape(1, d), *weights)


def _dil_kernel(q_ref, kp_ref, kc_ref, vp_ref, vc_ref, bias_ref, o_ref, lse_ref,
                k_scr, v_scr, *, rows):
    k_scr[0:BLOCK, :] = kp_ref[...]
    k_scr[BLOCK:, :] = kc_ref[...]
    v_scr[0:BLOCK, :] = vp_ref[...]
    v_scr[BLOCK:, :] = vc_ref[...]
    first_tile = pl.program_id(2) == 0
    heads = range(DIL_HEADS_PER_GROUP)
    chunks = range(rows // BLOCK)
    lane = lax.broadcasted_iota(jnp.int32, (BLOCK, DIL_OUT), 1)
    head_mask = [(lane >= h * HEAD_DIM) & (lane < (h + 1) * HEAD_DIM) for h in heads]
    key = lax.broadcasted_iota(jnp.int32, (len(heads) * BLOCK, 2 * BLOCK), 1)

    scores = []
    for c in chunks:
        q = q_ref[c * BLOCK:(c + 1) * BLOCK, :]
        q4 = jnp.concatenate([jnp.where(head_mask[h], q, jnp.zeros_like(q)) for h in heads], axis=0)
        s = _dot_nt(q4, k_scr[c * BLOCK:(c + 2) * BLOCK, :]) + bias_ref[...]
        if c == 0:
            s = jnp.where(first_tile & (key < BLOCK), NEG_INF, s)
        scores.append(s)
    tops = [jnp.max(s, axis=1, keepdims=True) for s in scores]
    probs = [jnp.exp(s - m) for s, m in zip(scores, tops)]
    sums = [jnp.sum(p, axis=1, keepdims=True) for p in probs]
    outs = [jnp.dot(p.astype(_BF16), v_scr[c * BLOCK:(c + 2) * BLOCK, :], preferred_element_type=_F32)
            for c, p in zip(chunks, probs)]
    for c in chunks:
        normed = outs[c] / sums[c]
        lse4 = tops[c] + jnp.log(sums[c])
        o = jnp.zeros((BLOCK, DIL_OUT), _F32)
        lse = jnp.zeros((BLOCK, DIL_OUT), _F32)
        for h in heads:
            o = jnp.where(head_mask[h], normed[h * BLOCK:(h + 1) * BLOCK, :], o)
            lse = jnp.where(head_mask[h], lse4[h * BLOCK:(h + 1) * BLOCK, :], lse)
        o_ref[c * BLOCK:(c + 1) * BLOCK, :] = o.astype(o_ref.dtype)
        lse_ref[c * BLOCK:(c + 1) * BLOCK, :] = lse


def _dilated_bias(group):
    window, dilation = DIL_GROUPS[group]
    n_back = window // dilation
    slopes = np.array([2.0 ** (-ALIBI_MAX_BIAS * (i + 1) / DIL_HEADS) for i in range(DIL_HEADS)],
                      dtype=np.float32).reshape(N_DIL, DIL_HEADS_PER_GROUP)[group]
    qi = np.arange(BLOCK)[:, None]
    ki = np.arange(2 * BLOCK)[None, :]
    rel = BLOCK + qi - ki
    valid = (rel >= 0) & (rel <= n_back)
    bias = -slopes[:, None, None] * (rel * dilation).astype(np.float32)[None]
    bias = np.where(valid[None], bias, np.float32(NEG_INF)).astype(np.float32)
    return jnp.asarray(bias.reshape(DIL_HEADS_PER_GROUP * BLOCK, 2 * BLOCK))


def _dilated_attention(qkv, group):
    bsz, dilation, sub_len, _ = qkv.shape
    rows = min(sub_len, 4 * BLOCK)
    n_tiles = sub_len // rows
    blocks_per_tile = rows // BLOCK

    def cur(n):
        return pl.BlockSpec((None, None, rows, DIL_OUT), lambda b, r, i: (b, r, i, n))

    def prev(n):
        return pl.BlockSpec((None, None, BLOCK, DIL_OUT),
                            lambda b, r, i: (b, r, jnp.maximum(i * blocks_per_tile - 1, 0), n))

    out_spec = pl.BlockSpec((None, None, rows, DIL_OUT), lambda b, r, i: (b, r, i, 0))
    return pl.pallas_call(
        functools.partial(_dil_kernel, rows=rows),
        out_shape=(jax.ShapeDtypeStruct((bsz, dilation, sub_len, DIL_OUT), _BF16),
                   jax.ShapeDtypeStruct((bsz, dilation, sub_len, DIL_OUT), _F32)),
        grid=(bsz, dilation, n_tiles),
        in_specs=[cur(0), prev(1), cur(1), prev(2), cur(2),
                  _const_spec((DIL_HEADS_PER_GROUP * BLOCK, 2 * BLOCK))],
        out_specs=(out_spec, out_spec),
        scratch_shapes=[pltpu.VMEM((rows + BLOCK, DIL_OUT), _BF16),
                        pltpu.VMEM((rows + BLOCK, DIL_OUT), _BF16)],
        compiler_params=_params("parallel", "parallel", "parallel"),
        name=f"dilated_attention_d{dilation}",
    )(qkv, qkv, qkv, qkv, qkv, _dilated_bias(group))


def _fox_kernel(q_ref, k_ref, kx_ref, v_ref, o_ref, qa_scr, s_scr, acc_scr, m_scr):
    pair = pl.program_id(1)
    i = pl.program_id(2)
    nq = q_ref.shape[1]
    half = nq // 2
    tap = lax.broadcasted_iota(jnp.int32, (LANES, nq), 0)
    zeros = jnp.zeros((HEAD_DIM, nq), _BF16)
    for hd in range(2):
        qa_scr[hd, 0:HEAD_DIM, :] = q_ref[0:HEAD_DIM, :] if hd == 0 else zeros
        qa_scr[hd, HEAD_DIM:LANES, :] = zeros if hd == 0 else q_ref[HEAD_DIM:LANES, :]
        first = DECAY_PIECES * (2 * pair + hd)
        qa_scr[hd, LANES:, :] = jnp.where((tap >= first) & (tap < first + DECAY_PIECES),
                                          -1.0, 0.0).astype(_BF16)
    acc_scr[...] = jnp.zeros_like(acc_scr)
    m_scr[...] = jnp.full_like(m_scr, NEG_INF)
    ones = jnp.ones((FOX_ACC_ROWS - HEAD_DIM, FOX_K), _BF16)

    def scores(j, lo):
        start = pl.multiple_of(j * FOX_K, FOX_K)
        k_aug = jnp.concatenate([k_ref[pl.ds(start, FOX_K), :], kx_ref[pl.ds(start, FOX_K), :]],
                                axis=1)
        return [jnp.dot(k_aug, qa_scr[hd, :, lo:nq], preferred_element_type=_F32)
                for hd in range(2)]

    def absorb(j, s, lo, masked):
        cols = slice(lo, nq)
        v_t = v_ref[j]
        if masked:
            key_pos = j * FOX_K + lax.broadcasted_iota(jnp.int32, s[0].shape, 0)
            q_pos = i * nq + lo + lax.broadcasted_iota(jnp.int32, s[0].shape, 1)
            s = [jnp.where(key_pos <= q_pos, x, NEG_INF) for x in s]
        m_old = [m_scr[hd, 0:1, cols] for hd in range(2)]
        m_new = [jnp.maximum(m, jnp.max(x, axis=0, keepdims=True)) for m, x in zip(m_old, s)]
        probs = [jnp.exp2(x - m).astype(_BF16) for x, m in zip(s, m_new)]
        alpha = [jnp.exp2(mo - mn) for mo, mn in zip(m_old, m_new)]
        for hd in range(2):
            v_aug = jnp.concatenate([v_t[hd * HEAD_DIM:(hd + 1) * HEAD_DIM, :], ones], axis=0)
            pv = jnp.dot(v_aug, probs[hd], preferred_element_type=_F32)
            acc_scr[hd, :, cols] = alpha[hd] * acc_scr[hd, :, cols] + pv
            m_scr[hd, :, cols] = jnp.broadcast_to(m_new[hd], (SUBLANES, nq - lo))

    def stash(slot, s):
        for hd in range(2):
            s_scr[slot, hd] = s[hd]

    stash(0, scores(0, 0))

    def body(t, carry):
        j = 2 * t
        stash(1, scores(j + 1, 0))
        absorb(j, [s_scr[0, 0], s_scr[0, 1]], 0, False)
        stash(0, scores(j + 2, 0))
        absorb(j + 1, [s_scr[1, 0], s_scr[1, 1]], 0, False)
        return carry

    lax.fori_loop(0, i, body, 0)
    last = scores(2 * i + 1, half)
    absorb(2 * i, [s_scr[0, 0], s_scr[0, 1]], 0, True)
    absorb(2 * i + 1, last, half, True)
    out_t = jnp.concatenate([acc_scr[hd, 0:HEAD_DIM, :] / acc_scr[hd, HEAD_DIM:HEAD_DIM + 1, :]
                             for hd in range(2)], axis=0)
    o_ref[...] = out_t.T.astype(o_ref.dtype)


def _forgetting_attention(q_t, k, k_decay, v_t):
    bsz, seq, _ = k.shape
    n_q = q_t.shape[1]
    assert q_t.shape[3] == FOX_Q == 2 * FOX_K and v_t.shape[3] == FOX_K
    return pl.pallas_call(
        _fox_kernel,
        out_shape=jax.ShapeDtypeStruct((bsz, seq, FOX_WIDTH), _BF16),
        grid=(bsz, FOX_PAIRS, n_q),
        in_specs=[pl.BlockSpec((None, None, LANES, FOX_Q), lambda b, p, i: (b, i, p, 0)),
                  pl.BlockSpec((None, seq, LANES), lambda b, p, i: (b, 0, p)),
                  pl.BlockSpec((None, seq, LANES), lambda b, p, i: (b, 0, 0)),
                  pl.BlockSpec((None, seq // FOX_K, LANES, FOX_K), lambda b, p, i: (b, 0, p, 0))],
        out_specs=pl.BlockSpec((None, FOX_Q, LANES), lambda b, p, i: (b, i, p)),
        scratch_shapes=[pltpu.VMEM((2, 2 * LANES, FOX_Q), _BF16),
                        pltpu.VMEM((2, 2, FOX_K, FOX_Q), _F32),
                        pltpu.VMEM((2, FOX_ACC_ROWS, FOX_Q), _F32),
                        pltpu.VMEM((2, SUBLANES, FOX_Q), _F32)],
        compiler_params=_params("parallel", "parallel", "arbitrary"),
        name="forgetting_attention",
    )(q_t, k, k_decay, v_t)


def _merge_kernel(x_ref, gt_ref, o1_ref, o4_ref, o16_ref, l1_ref, l4_ref, l16_ref, yb_ref,
                  ga_ref, gb_ref, wa_ref, wb_ref, wo_ref, out_ref, o_scr, l_scr):
    rows = x_ref.shape[0]
    for g, (dilation, o_ref, l_ref) in enumerate(((4, o4_ref, l4_ref), (16, o16_ref, l16_ref))):
        n = rows // dilation
        for r in range(dilation):
            o_res = o_ref[r].astype(_F32)
            l_res = l_ref[r]
            for cb in range(DIL_OUT // LANES):
                lanes = slice(cb * LANES, (cb + 1) * LANES)
                o_scr[g, cb, pl.ds(r, n, stride=dilation), :] = o_res[:, lanes]
                l_scr[g, cb, pl.ds(r, n, stride=dilation), :] = l_res[:, lanes]

    def token_order(scr, g):
        return jnp.concatenate([scr[g, cb] for cb in range(DIL_OUT // LANES)], axis=1)

    l1, l2, l3 = l1_ref[0], token_order(l_scr, 0), token_order(l_scr, 1)
    top = jnp.maximum(jnp.maximum(l1, l2), l3)
    e1, e2, e3 = jnp.exp(l1 - top), jnp.exp(l2 - top), jnp.exp(l3 - top)
    y_a = (e1 * o1_ref[0].astype(_F32) + e2 * token_order(o_scr, 0)
           + e3 * token_order(o_scr, 1)) / (e1 + e2 + e3)
    br_a = jnp.dot(y_a.astype(_BF16), wa_ref[...], preferred_element_type=_F32)
    br_b = jnp.dot(yb_ref[...], wb_ref[...], preferred_element_type=_F32)
    merged = ga_ref[...].astype(_F32) * br_a + gb_ref[...].astype(_F32) * br_b
    mixed = jnp.dot(merged.astype(_BF16), wo_ref[...], preferred_element_type=_F32)
    out_ref[...] = x_ref[...] + gt_ref[...] * mixed


def _merge(x, gate, dil_outs, dil_lses, y_b, gates, w_branch_a, w_branch_b, w_out):
    bsz, seq, d = x.shape
    rows = PROJ_ROWS
    row = lambda n, c=0: pl.BlockSpec((None, rows, n), lambda b, i: (b, i, c))
    dil_specs = [pl.BlockSpec((None, dilation, rows // dilation, DIL_OUT), lambda b, i: (b, 0, i, 0))
                 for _, dilation in DIL_GROUPS]
    return pl.pallas_call(
        _merge_kernel,
        out_shape=jax.ShapeDtypeStruct(x.shape, _F32),
        grid=(bsz, seq // rows),
        in_specs=[row(d), pl.BlockSpec((None, 1, d), lambda b, i: (b, 0, 0)),
                  *dil_specs, *dil_specs,
                  row(FOX_WIDTH), row(d, 0), row(d, 1),
                  _const_spec(w_branch_a.shape), _const_spec(w_branch_b.shape),
                  _const_spec(w_out.shape)],
        out_specs=row(d),
        scratch_shapes=[pltpu.VMEM((N_DIL - 1, DIL_OUT // LANES, rows, LANES), _F32),
                        pltpu.VMEM((N_DIL - 1, DIL_OUT // LANES, rows, LANES), _F32)],
        compiler_params=_params("parallel", "parallel"),
        name="branch_merge",
    )(x, gate, *dil_outs, *dil_lses, y_b, gates, gates, w_branch_a, w_branch_b, w_out)


def _decay_columns(values):
    rep = jnp.repeat(values, DECAY_PIECES, axis=-1)
    pad = [(0, 0)] * (values.ndim - 1) + [(0, LANES - rep.shape[-1])]
    return jnp.pad(rep, pad)


def kernel(x, c, ada_w, ada_b, norm_ffn1, ffn1_w_gate, ffn1_w_up, ffn1_w_down, norm_mix, w_in,
           forget_bias, w_branch_a, w_branch_b, w_out, norm_ffn2, ffn2_w_gate, ffn2_w_up,
           ffn2_w_down, norm_final):
    bsz, seq, d = x.shape
    depth = ada_w.shape[0]
    if depth < 1:
        raise ValueError("depth must be positive")
    inv_sqrt_e = 1.0 / math.sqrt(HEAD_DIM)
    offsets = [int(o) for o in np.cumsum(IN_SIZES)[:-1]]
    for layer in range(depth):
        mod = _ada_modulation(c, ada_w[layer], ada_b[layer])
        sh1, sc1, gt1, sh2, sc2, gt2, sh3, sc3, gt3 = (
            m.reshape(bsz, 1, d) for m in jnp.split(mod, N_MOD, axis=-1))

        x = _ffn(x, sh1, sc1, gt1, norm_ffn1[layer], ffn1_w_gate[layer], ffn1_w_up[layer],
                 ffn1_w_down[layer])

        wqa, wka, wva, wqb, wkb, wvb, wf, wga, wgb = jnp.split(w_in[layer], offsets, axis=-1)
        w_dil = []
        for g in range(N_DIL):
            cols = slice(g * DIL_OUT, (g + 1) * DIL_OUT)
            w_dil.append(jnp.concatenate([wqa[:, cols] * inv_sqrt_e, wka[:, cols], wva[:, cols]],
                                         axis=-1).astype(_BF16))
        w_gates = jnp.concatenate([wga, wgb], axis=-1).astype(_BF16)
        qkv1, qkv4, qkv16, q_t, k_b, v_t, k_decay, gates = _mixer_projection(
            x, sh2, sc2, norm_mix[layer], w_dil, (wqb * (inv_sqrt_e * LOG2_E)).T.astype(_BF16),
            wkb.astype(_BF16), wvb.T.astype(_BF16), w_gates,
            _decay_columns(wf).astype(_BF16), _decay_columns(forget_bias[layer][None, :]))

        dil = [_dilated_attention(qkv, g) for g, qkv in enumerate((qkv1, qkv4, qkv16))]
        y_b = _forgetting_attention(q_t, k_b, k_decay, v_t)
        x = _merge(x, gt2, [o for o, _ in dil], [l for _, l in dil], y_b, gates,
                   w_branch_a[layer].astype(_BF16), w_branch_b[layer].astype(_BF16),
                   w_out[layer].astype(_BF16))

        last = layer == depth - 1
        x = _ffn(x, sh3, sc3, gt3, norm_ffn2[layer], ffn2_w_gate[layer], ffn2_w_up[layer],
                 ffn2_w_down[layer], final_gain=norm_final if last else None)
    return x
```

```python
import functools
import math

import numpy as np
import jax
import jax.numpy as jnp
from jax import lax
from jax.experimental import pallas as pl
from jax.experimental.pallas import tpu as pltpu

D_MODEL = 1024
HEAD_DIM = 64
DIL_GROUPS = ((128, 1), (512, 4), (2048, 16))
N_DIL = len(DIL_GROUPS)
DIL_HEADS_PER_GROUP = 4
DIL_HEADS = N_DIL * DIL_HEADS_PER_GROUP
DIL_WIDTH = DIL_HEADS * HEAD_DIM
DIL_OUT = DIL_HEADS_PER_GROUP * HEAD_DIM
FOX_HEADS = 8
FOX_PAIRS = FOX_HEADS // 2
FOX_WIDTH = FOX_HEADS * HEAD_DIM
BLOCK = 128
N_MOD = 9
RMS_EPS = 1e-6
ALIBI_MAX_BIAS = 8.0
NEG_INF = -1e30
IN_SIZES = (DIL_WIDTH, DIL_WIDTH, DIL_WIDTH, FOX_WIDTH, FOX_WIDTH, FOX_WIDTH,
            FOX_HEADS, D_MODEL, D_MODEL)

LANES = 128
SUBLANES = 8
VMEM_LIMIT_BYTES = 56 * 1024 * 1024

FFN_ROWS = 512
FFN_CHUNK = 256
PROJ_ROWS = 512
PROJ_CHUNK = 256
FOX_Q = 1024
FOX_K = 256
FOX_ACC_ROWS = HEAD_DIM + 16
DECAY_PIECES = 3
LOG2_E = math.log2(math.e)
ADA_COLS = 1024

_F32 = jnp.float32
_BF16 = jnp.bfloat16


def _const_spec(shape):
    zeros = (0,) * len(shape)
    return pl.BlockSpec(shape, lambda *_: zeros, pipeline_mode=pl.Buffered(1))


def _params(*semantics):
    return pltpu.CompilerParams(dimension_semantics=semantics,
                                vmem_limit_bytes=VMEM_LIMIT_BYTES)


def _modulated_norm(x, gain, shift, scale):
    ms = jnp.mean(x * x, axis=-1, keepdims=True)
    y = x * lax.rsqrt(ms + RMS_EPS) * gain
    return y * (1.0 + scale) + shift


def _dot_nt(a, b):
    return lax.dot_general(a, b, (((1,), (1,)), ((), ())), preferred_element_type=_F32)


def _ada_kernel(c_ref, w_ref, b_ref, o_ref):
    c = c_ref[...]
    act = c * jax.nn.sigmoid(c)
    o_ref[...] = jnp.dot(act, w_ref[...], preferred_element_type=_F32,
                         precision=lax.Precision.HIGHEST) + b_ref[...]


def _ada_modulation(c, w, b):
    bsz, d = c.shape
    n = w.shape[1]
    return pl.pallas_call(
        _ada_kernel,
        out_shape=jax.ShapeDtypeStruct((bsz, n), _F32),
        grid=(n // ADA_COLS,),
        in_specs=[pl.BlockSpec((bsz, d), lambda j: (0, 0)),
                  pl.BlockSpec((d, ADA_COLS), lambda j: (0, j)),
                  pl.BlockSpec((1, ADA_COLS), lambda j: (0, j))],
        out_specs=pl.BlockSpec((bsz, ADA_COLS), lambda j: (0, j)),
        compiler_params=_params("arbitrary"),
        name="ada_modulation",
    )(c, w, b.reshape(1, n))


def _ffn_kernel(*refs, n_chunks, final_norm):
    if final_norm:
        (x_ref, sh_ref, sc_ref, gt_ref, g_ref, wg_ref, wu_ref, wd_ref, gf_ref,
         o_ref, h_scr, acc_scr) = refs
    else:
        (x_ref, sh_ref, sc_ref, gt_ref, g_ref, wg_ref, wu_ref, wd_ref,
         o_ref, h_scr, acc_scr) = refs
    h = _modulated_norm(x_ref[...], g_ref[...], sh_ref[...], sc_ref[...])
    h_scr[...] = h.astype(_BF16)
    for j in range(n_chunks):
        hb = h_scr[...]
        cols = slice(j * FFN_CHUNK, (j + 1) * FFN_CHUNK)
        g = jnp.dot(hb, wg_ref[:, cols], preferred_element_type=_F32)
        u = jnp.dot(hb, wu_ref[:, cols], preferred_element_type=_F32)
        acc_scr[:, cols] = (g * jax.nn.sigmoid(g) * u).astype(_BF16)
    down = jnp.dot(acc_scr[...], wd_ref[...], preferred_element_type=_F32)
    out = x_ref[...] + 0.5 * gt_ref[...] * down
    if final_norm:
        ms = jnp.mean(out * out, axis=-1, keepdims=True)
        out = out * lax.rsqrt(ms + RMS_EPS) * gf_ref[...]
    o_ref[...] = out


def _ffn(x, shift, scale, gate, gain, w_gate, w_up, w_down, final_gain=None):
    bsz, seq, d = x.shape
    d_ff = w_gate.shape[1]
    n_chunks = d_ff // FFN_CHUNK
    wg = w_gate.astype(_BF16)
    wu = w_up.astype(_BF16)
    wd = w_down.astype(_BF16)
    row_spec = pl.BlockSpec((None, FFN_ROWS, d), lambda b, i: (b, i, 0))
    mod_spec = pl.BlockSpec((None, 1, d), lambda b, i: (b, 0, 0))
    in_specs = [row_spec, mod_spec, mod_spec, mod_spec, _const_spec((1, d)),
                _const_spec(wg.shape), _const_spec(wu.shape), _const_spec(wd.shape)]
    args = [x, shift, scale, gate, gain.reshape(1, d), wg, wu, wd]
    if final_gain is not None:
        in_specs.append(_const_spec((1, d)))
        args.append(final_gain.reshape(1, d))
    return pl.pallas_call(
        functools.partial(_ffn_kernel, n_chunks=n_chunks, final_norm=final_gain is not None),
        out_shape=jax.ShapeDtypeStruct(x.shape, _F32),
        grid=(bsz, seq // FFN_ROWS),
        in_specs=in_specs,
        out_specs=row_spec,
        scratch_shapes=[pltpu.VMEM((FFN_ROWS, d), _BF16), pltpu.VMEM((FFN_ROWS, d_ff), _BF16)],
        compiler_params=_params("parallel", "parallel"),
        name="ffn_final" if final_gain is not None else "ffn",
    )(*args)


def _row_cumsum(x):
    n = x.shape[0]
    row = lax.broadcasted_iota(jnp.int32, x.shape, 0)
    shift = 1
    while shift < n:
        x = x + jnp.where(row >= shift, pltpu.roll(x, shift, 0), 0.0)
        shift *= 2
    return x


def _proj_kernel(x_ref, sh_ref, sc_ref, g_ref, wd1_ref, wd4_ref, wd16_ref, wqv_ref, wk_ref,
                 wgate_ref, wf_ref, fb_ref,
                 od1_ref, od4_ref, od16_ref, oq_ref, ok_ref, ov_ref, okx_ref, og_ref,
                 hf_scr, h_scr, hp4_scr, hp16_scr, carry_scr):
    rows = x_ref.shape[0]

    @pl.when(pl.program_id(1) == 0)
    def _():
        carry_scr[...] = jnp.zeros_like(carry_scr)

    h = _modulated_norm(x_ref[...], g_ref[...], sh_ref[...], sc_ref[...])
    h_scr[...] = h.astype(_BF16)

    def dilated_group(dilation, hp_scr, w_ref, o_ref):
        n = rows // dilation
        for c in range(w_ref.shape[1] // PROJ_CHUNK):
            cols = slice(c * PROJ_CHUNK, (c + 1) * PROJ_CHUNK)
            y = jnp.dot(hp_scr[...], w_ref[:, cols], preferred_element_type=_F32).astype(_BF16)
            for r in range(dilation):
                o_ref[r, :, cols] = y[r * n:(r + 1) * n, :]

    dilated_group(1, h_scr, wd1_ref, od1_ref)

    qv_t = _dot_nt(wqv_ref[...], h_scr[...]).astype(_BF16)
    oq_ref[...] = qv_t[0:FOX_WIDTH, :]
    for kb in range(rows // FOX_K):
        ov_ref[kb] = qv_t[FOX_WIDTH:, kb * FOX_K:(kb + 1) * FOX_K]
    for c in range(FOX_WIDTH // PROJ_CHUNK):
        feat = slice(c * PROJ_CHUNK, (c + 1) * PROJ_CHUNK)
        ok_ref[:, feat] = jnp.dot(h_scr[...], wk_ref[:, feat],
                                  preferred_element_type=_F32).astype(_BF16)

    f = jnp.dot(h_scr[...], wf_ref[...], preferred_element_type=_F32) + fb_ref[...]
    log_sig = jnp.minimum(f, 0.0) - jnp.log1p(jnp.exp(-jnp.abs(f)))
    cum = _row_cumsum(log_sig) + carry_scr[0:1, :]
    carry_scr[...] = jnp.broadcast_to(cum[rows - 1:rows, :], carry_scr.shape)
    cum = cum * LOG2_E
    p1 = cum.astype(_BF16)
    r1 = cum - p1.astype(_F32)
    p2 = r1.astype(_BF16)
    p3 = (r1 - p2.astype(_F32)).astype(_BF16)
    lane = lax.broadcasted_iota(jnp.int32, cum.shape, 1)
    piece = lane % DECAY_PIECES
    pieces = jnp.where(piece == 0, p1.astype(_F32),
                       jnp.where(piece == 1, p2.astype(_F32), p3.astype(_F32)))
    okx_ref[...] = jnp.where(lane < DECAY_PIECES * FOX_HEADS, pieces, 0.0).astype(_BF16)

    for c in range(wgate_ref.shape[1] // PROJ_CHUNK):
        cols = slice(c * PROJ_CHUNK, (c + 1) * PROJ_CHUNK)
        y = jnp.dot(h_scr[...], wgate_ref[:, cols], preferred_element_type=_F32)
        og_ref[:, cols] = jax.nn.sigmoid(y).astype(_BF16)

    for cb in range(hf_scr.shape[0]):
        hf_scr[cb] = h[:, cb * LANES:(cb + 1) * LANES]
    for dilation, hp_scr, w_ref, o_ref in ((4, hp4_scr, wd4_ref, od4_ref),
                                           (16, hp16_scr, wd16_ref, od16_ref)):
        n = rows // dilation
        for r in range(dilation):
            for cb in range(hf_scr.shape[0]):
                hp_scr[r * n:(r + 1) * n, cb * LANES:(cb + 1) * LANES] = (
                    hf_scr[cb, pl.ds(r, n, stride=dilation), :].astype(_BF16))
        dilated_group(dilation, hp_scr, w_ref, o_ref)


def _mixer_projection(x, shift, scale, gain, w_dil, w_qv_t, w_k, w_gates, w_f, f_bias):
    bsz, seq, d = x.shape
    rows = PROJ_ROWS
    n_tiles = seq // rows
    tile = lambda b, i: (b, i, 0)
    row = lambda n: pl.BlockSpec((None, rows, n), tile)
    mod_spec = pl.BlockSpec((None, 1, d), lambda b, i: (b, 0, 0))
    dil_shapes, dil_specs = [], []
    for _, dilation in DIL_GROUPS:
        dil_shapes.append(jax.ShapeDtypeStruct((bsz, dilation, seq // dilation, 3 * DIL_OUT), _BF16))
        dil_specs.append(pl.BlockSpec((None, dilation, rows // dilation, 3 * DIL_OUT),
                                      lambda b, i: (b, 0, i, 0)))
    out_shape = (*dil_shapes,
                 jax.ShapeDtypeStruct((bsz, n_tiles, FOX_WIDTH, rows), _BF16),
                 jax.ShapeDtypeStruct((bsz, seq, FOX_WIDTH), _BF16),
                 jax.ShapeDtypeStruct((bsz, seq // FOX_K, FOX_WIDTH, FOX_K), _BF16),
                 jax.ShapeDtypeStruct((bsz, seq, LANES), _BF16),
                 jax.ShapeDtypeStruct((bsz, seq, w_gates.shape[1]), _BF16))
    out_specs = (*dil_specs,
                 pl.BlockSpec((None, None, FOX_WIDTH, rows), lambda b, i: (b, i, 0, 0)),
                 row(FOX_WIDTH),
                 pl.BlockSpec((None, rows // FOX_K, FOX_WIDTH, FOX_K), lambda b, i: (b, i, 0, 0)),
                 row(LANES),
                 row(w_gates.shape[1]))
    weights = [*w_dil, w_qv_t, w_k, w_gates, w_f, f_bias]
    return pl.pallas_call(
        _proj_kernel,
        out_shape=out_shape,
        grid=(bsz, n_tiles),
        in_specs=[row(d), mod_spec, mod_spec, _const_spec((1, d)),
                  *[_const_spec(w.shape) for w in weights]],
        out_specs=out_specs,
        scratch_shapes=[pltpu.VMEM((d // LANES, rows, LANES), _F32), pltpu.VMEM((rows, d), _BF16),
                        pltpu.VMEM((rows, d), _BF16), pltpu.VMEM((rows, d), _BF16),
                        pltpu.VMEM((SUBLANES, LANES), _F32)],
        compiler_params=_params("parallel", "arbitrary"),
        name="mixer_projection",
    )(x, shift, scale, gain.reshape(1, d), *weights)


def _dil_kernel(q_ref, kp_ref, kc_ref, vp_ref, vc_ref, bias_ref, o_ref, lse_ref,
                k_scr, v_scr, *, rows):
    k_scr[0:BLOCK, :] = kp_ref[...]
    k_scr[BLOCK:, :] = kc_ref[...]
    v_scr[0:BLOCK, :] = vp_ref[...]
    v_scr[BLOCK:, :] = vc_ref[...]
    first_tile = pl.program_id(2) == 0
    heads = range(DIL_HEADS_PER_GROUP)
    chunks = range(rows // BLOCK)
    lane = lax.broadcasted_iota(jnp.int32, (BLOCK, DIL_OUT), 1)
    head_mask = [(lane >= h * HEAD_DIM) & (lane < (h + 1) * HEAD_DIM) for h in heads]
    key = lax.broadcasted_iota(jnp.int32, (len(heads) * BLOCK, 2 * BLOCK), 1)

    scores = []
    for c in chunks:
        q = q_ref[c * BLOCK:(c + 1) * BLOCK, :]
        q4 = jnp.concatenate([jnp.where(head_mask[h], q, jnp.zeros_like(q)) for h in heads], axis=0)
        s = _dot_nt(q4, k_scr[c * BLOCK:(c + 2) * BLOCK, :]) + bias_ref[...]
        if c == 0:
            s = jnp.where(first_tile & (key < BLOCK), NEG_INF, s)
        scores.append(s)
    tops = [jnp.max(s, axis=1, keepdims=True) for s in scores]
    probs = [jnp.exp(s - m) for s, m in zip(scores, tops)]
    sums = [jnp.sum(p, axis=1, keepdims=True) for p in probs]
    outs = [jnp.dot(p.astype(_BF16), v_scr[c * BLOCK:(c + 2) * BLOCK, :], preferred_element_type=_F32)
            for c, p in zip(chunks, probs)]
    for c in chunks:
        normed = outs[c] / sums[c]
        lse4 = tops[c] + jnp.log(sums[c])
        o = jnp.zeros((BLOCK, DIL_OUT), _F32)
        lse = jnp.zeros((BLOCK, DIL_OUT), _F32)
        for h in heads:
            o = jnp.where(head_mask[h], normed[h * BLOCK:(h + 1) * BLOCK, :], o)
            lse = jnp.where(head_mask[h], lse4[h * BLOCK:(h + 1) * BLOCK, :], lse)
        o_ref[c * BLOCK:(c + 1) * BLOCK, :] = o.astype(o_ref.dtype)
        lse_ref[c * BLOCK:(c + 1) * BLOCK, :] = lse


def _dilated_bias(group):
    window, dilation = DIL_GROUPS[group]
    n_back = window // dilation
    slopes = np.array([2.0 ** (-ALIBI_MAX_BIAS * (i + 1) / DIL_HEADS) for i in range(DIL_HEADS)],
                      dtype=np.float32).reshape(N_DIL, DIL_HEADS_PER_GROUP)[group]
    qi = np.arange(BLOCK)[:, None]
    ki = np.arange(2 * BLOCK)[None, :]
    rel = BLOCK + qi - ki
    valid = (rel >= 0) & (rel <= n_back)
    bias = -slopes[:, None, None] * (rel * dilation).astype(np.float32)[None]
    bias = np.where(valid[None], bias, np.float32(NEG_INF)).astype(np.float32)
    return jnp.asarray(bias.reshape(DIL_HEADS_PER_GROUP * BLOCK, 2 * BLOCK))


def _dilated_attention(qkv, group):
    bsz, dilation, sub_len, _ = qkv.shape
    rows = min(sub_len, 4 * BLOCK)
    n_tiles = sub_len // rows
    blocks_per_tile = rows // BLOCK

    def cur(n):
        return pl.BlockSpec((None, None, rows, DIL_OUT), lambda b, r, i: (b, r, i, n))

    def prev(n):
        return pl.BlockSpec((None, None, BLOCK, DIL_OUT),
                            lambda b, r, i: (b, r, jnp.maximum(i * blocks_per_tile - 1, 0), n))

    out_spec = pl.BlockSpec((None, None, rows, DIL_OUT), lambda b, r, i: (b, r, i, 0))
    return pl.pallas_call(
        functools.partial(_dil_kernel, rows=rows),
        out_shape=(jax.ShapeDtypeStruct((bsz, dilation, sub_len, DIL_OUT), _BF16),
                   jax.ShapeDtypeStruct((bsz, dilation, sub_len, DIL_OUT), _F32)),
        grid=(bsz, dilation, n_tiles),
        in_specs=[cur(0), prev(1), cur(1), prev(2), cur(2),
                  _const_spec((DIL_HEADS_PER_GROUP * BLOCK, 2 * BLOCK))],
        out_specs=(out_spec, out_spec),
        scratch_shapes=[pltpu.VMEM((rows + BLOCK, DIL_OUT), _BF16),
                        pltpu.VMEM((rows + BLOCK, DIL_OUT), _BF16)],
        compiler_params=_params("parallel", "parallel", "parallel"),
        name=f"dilated_attention_d{dilation}",
    )(qkv, qkv, qkv, qkv, qkv, _dilated_bias(group))


def _fox_kernel(q_ref, k_ref, kx_ref, v_ref, o_ref, qa_scr, s_scr, acc_scr, m_scr):
    pair = pl.program_id(1)
    i = pl.program_id(2)
    q_tiles, _, tile = q_ref.shape
    nq = q_tiles * tile
    n_diag = nq // FOX_K
    tap = lax.broadcasted_iota(jnp.int32, (LANES, nq), 0)
    zeros = jnp.zeros((HEAD_DIM, tile), _BF16)
    for hd in range(2):
        for t in range(q_tiles):
            cols = slice(t * tile, (t + 1) * tile)
            qa_scr[hd, 0:HEAD_DIM, cols] = q_ref[t, 0:HEAD_DIM, :] if hd == 0 else zeros
            qa_scr[hd, HEAD_DIM:LANES, cols] = zeros if hd == 0 else q_ref[t, HEAD_DIM:LANES, :]
        first = DECAY_PIECES * (2 * pair + hd)
        qa_scr[hd, LANES:, :] = jnp.where((tap >= first) & (tap < first + DECAY_PIECES),
                                          -1.0, 0.0).astype(_BF16)
    acc_scr[...] = jnp.zeros_like(acc_scr)
    m_scr[...] = jnp.full_like(m_scr, NEG_INF)
    ones = jnp.ones((FOX_ACC_ROWS - HEAD_DIM, FOX_K), _BF16)

    def scores(j, lo):
        start = pl.multiple_of(j * FOX_K, FOX_K)
        k_aug = jnp.concatenate([k_ref[pl.ds(start, FOX_K), :], kx_ref[pl.ds(start, FOX_K), :]],
                                axis=1)
        return [jnp.dot(k_aug, qa_scr[hd, :, lo:nq], preferred_element_type=_F32)
                for hd in range(2)]

    def absorb(j, s, lo, masked):
        cols = slice(lo, nq)
        v_t = v_ref[j]
        if masked:
            keep = (lax.broadcasted_iota(jnp.int32, (FOX_K, FOX_K), 0)
                    <= lax.broadcasted_iota(jnp.int32, (FOX_K, FOX_K), 1))
            diag = [jnp.where(keep, x[:, :FOX_K], NEG_INF) for x in s]
            if nq - lo > FOX_K:
                s = [jnp.concatenate([dg, x[:, FOX_K:]], axis=1) for dg, x in zip(diag, s)]
            else:
                s = diag
        m_old = [m_scr[hd, 0:1, cols] for hd in range(2)]
        m_new = [jnp.maximum(m, jnp.max(x, axis=0, keepdims=True)) for m, x in zip(m_old, s)]
        probs = [jnp.exp2(x - m).astype(_BF16) for x, m in zip(s, m_new)]
        alpha = [jnp.exp2(mo - mn) for mo, mn in zip(m_old, m_new)]
        for hd in range(2):
            v_aug = jnp.concatenate([v_t[hd * HEAD_DIM:(hd + 1) * HEAD_DIM, :], ones], axis=0)
            pv = jnp.dot(v_aug, probs[hd], preferred_element_type=_F32)
            acc_scr[hd, :, cols] = alpha[hd] * acc_scr[hd, :, cols] + pv
            m_scr[hd, :, cols] = jnp.broadcast_to(m_new[hd], (SUBLANES, nq - lo))

    def stash(slot, s, lo):
        for hd in range(2):
            s_scr[slot, hd, :, lo:nq] = s[hd]

    def stashed(slot, lo):
        return [s_scr[slot, hd, :, lo:nq] for hd in range(2)]

    stash(0, scores(0, 0), 0)

    def body(t, carry):
        j = n_diag * t
        for d in range(n_diag):
            stash((d + 1) % 2, scores(j + d + 1, 0), 0)
            absorb(j + d, stashed(d % 2, 0), 0, False)
        return carry

    lax.fori_loop(0, i, body, 0)
    j = n_diag * i
    for d in range(n_diag):
        if d + 1 < n_diag:
            stash((d + 1) % 2, scores(j + d + 1, (d + 1) * FOX_K), (d + 1) * FOX_K)
        absorb(j + d, stashed(d % 2, d * FOX_K), d * FOX_K, True)
    out_t = jnp.concatenate([acc_scr[hd, 0:HEAD_DIM, :] / acc_scr[hd, HEAD_DIM:HEAD_DIM + 1, :]
                             for hd in range(2)], axis=0)
    o_ref[...] = out_t.T.astype(o_ref.dtype)


def _forgetting_attention(q_t, k, k_decay, v_t):
    bsz, seq, _ = k.shape
    tile = q_t.shape[3]
    assert FOX_Q % tile == 0 and (FOX_Q // FOX_K) % 2 == 0 and v_t.shape[3] == FOX_K
    return pl.pallas_call(
        _fox_kernel,
        out_shape=jax.ShapeDtypeStruct((bsz, seq, FOX_WIDTH), _BF16),
        grid=(bsz, FOX_PAIRS, seq // FOX_Q),
        in_specs=[pl.BlockSpec((None, FOX_Q // tile, LANES, tile), lambda b, p, i: (b, i, p, 0)),
                  pl.BlockSpec((None, seq, LANES), lambda b, p, i: (b, 0, p)),
                  pl.BlockSpec((None, seq, LANES), lambda b, p, i: (b, 0, 0)),
                  pl.BlockSpec((None, seq // FOX_K, LANES, FOX_K), lambda b, p, i: (b, 0, p, 0))],
        out_specs=pl.BlockSpec((None, FOX_Q, LANES), lambda b, p, i: (b, i, p)),
        scratch_shapes=[pltpu.VMEM((2, 2 * LANES, FOX_Q), _BF16),
                        pltpu.VMEM((2, 2, FOX_K, FOX_Q), _F32),
                        pltpu.VMEM((2, FOX_ACC_ROWS, FOX_Q), _F32),
                        pltpu.VMEM((2, SUBLANES, FOX_Q), _F32)],
        compiler_params=_params("parallel", "parallel", "arbitrary"),
        name="forgetting_attention",
    )(q_t, k, k_decay, v_t)


def _merge_kernel(x_ref, gt_ref, o1_ref, o4_ref, o16_ref, l1_ref, l4_ref, l16_ref, yb_ref,
                  ga_ref, gb_ref, wa_ref, wb_ref, wo_ref, out_ref, o_scr, l_scr):
    rows = x_ref.shape[0]
    for g, (dilation, o_ref, l_ref) in enumerate(((4, o4_ref, l4_ref), (16, o16_ref, l16_ref))):
        n = rows // dilation
        for r in range(dilation):
            o_res = o_ref[r].astype(_F32)
            l_res = l_ref[r]
            for cb in range(DIL_OUT // LANES):
                lanes = slice(cb * LANES, (cb + 1) * LANES)
                o_scr[g, cb, pl.ds(r, n, stride=dilation), :] = o_res[:, lanes]
                l_scr[g, cb, pl.ds(r, n, stride=dilation), :] = l_res[:, lanes]

    def token_order(scr, g):
        return jnp.concatenate([scr[g, cb] for cb in range(DIL_OUT // LANES)], axis=1)

    l1, l2, l3 = l1_ref[0], token_order(l_scr, 0), token_order(l_scr, 1)
    top = jnp.maximum(jnp.maximum(l1, l2), l3)
    e1, e2, e3 = jnp.exp(l1 - top), jnp.exp(l2 - top), jnp.exp(l3 - top)
    y_a = (e1 * o1_ref[0].astype(_F32) + e2 * token_order(o_scr, 0)
           + e3 * token_order(o_scr, 1)) / (e1 + e2 + e3)
    br_a = jnp.dot(y_a.astype(_BF16), wa_ref[...], preferred_element_type=_F32)
    br_b = jnp.dot(yb_ref[...], wb_ref[...], preferred_element_type=_F32)
    merged = ga_ref[...].astype(_F32) * br_a + gb_ref[...].astype(_F32) * br_b
    mixed = jnp.dot(merged.astype(_BF16), wo_ref[...], preferred_element_type=_F32)
    out_ref[...] = x_ref[...] + gt_ref[...] * mixed


def _merge(x, gate, dil_outs, dil_lses, y_b, gates, w_branch_a, w_branch_b, w_out):
    bsz, seq, d = x.shape
    rows = PROJ_ROWS
    row = lambda n, c=0: pl.BlockSpec((None, rows, n), lambda b, i: (b, i, c))
    dil_specs = [pl.BlockSpec((None, dilation, rows // dilation, DIL_OUT), lambda b, i: (b, 0, i, 0))
                 for _, dilation in DIL_GROUPS]
    return pl.pallas_call(
        _merge_kernel,
        out_shape=jax.ShapeDtypeStruct(x.shape, _F32),
        grid=(bsz, seq // rows),
        in_specs=[row(d), pl.BlockSpec((None, 1, d), lambda b, i: (b, 0, 0)),
                  *dil_specs, *dil_specs,
                  row(FOX_WIDTH), row(d, 0), row(d, 1),
                  _const_spec(w_branch_a.shape), _const_spec(w_branch_b.shape),
                  _const_spec(w_out.shape)],
        out_specs=row(d),
        scratch_shapes=[pltpu.VMEM((N_DIL - 1, DIL_OUT // LANES, rows, LANES), _F32),
                        pltpu.VMEM((N_DIL - 1, DIL_OUT // LANES, rows, LANES), _F32)],
        compiler_params=_params("parallel", "parallel"),
        name="branch_merge",
    )(x, gate, *dil_outs, *dil_lses, y_b, gates, gates, w_branch_a, w_branch_b, w_out)


def _decay_columns(values):
    rep = jnp.repeat(values, DECAY_PIECES, axis=-1)
    pad = [(0, 0)] * (values.ndim - 1) + [(0, LANES - rep.shape[-1])]
    return jnp.pad(rep, pad)


def kernel(x, c, ada_w, ada_b, norm_ffn1, ffn1_w_gate, ffn1_w_up, ffn1_w_down, norm_mix, w_in,
           forget_bias, w_branch_a, w_branch_b, w_out, norm_ffn2, ffn2_w_gate, ffn2_w_up,
           ffn2_w_down, norm_final):
    bsz, seq, d = x.shape
    depth = ada_w.shape[0]
    if depth < 1:
        raise ValueError("depth must be positive")
    inv_sqrt_e = 1.0 / math.sqrt(HEAD_DIM)
    offsets = [int(o) for o in np.cumsum(IN_SIZES)[:-1]]
    for layer in range(depth):
        mod = _ada_modulation(c, ada_w[layer], ada_b[layer])
        sh1, sc1, gt1, sh2, sc2, gt2, sh3, sc3, gt3 = (
            m.reshape(bsz, 1, d) for m in jnp.split(mod, N_MOD, axis=-1))

        x = _ffn(x, sh1, sc1, gt1, norm_ffn1[layer], ffn1_w_gate[layer], ffn1_w_up[layer],
                 ffn1_w_down[layer])

        wqa, wka, wva, wqb, wkb, wvb, wf, wga, wgb = jnp.split(w_in[layer], offsets, axis=-1)
        w_dil = []
        for g in range(N_DIL):
            cols = slice(g * DIL_OUT, (g + 1) * DIL_OUT)
            w_dil.append(jnp.concatenate([wqa[:, cols] * inv_sqrt_e, wka[:, cols], wva[:, cols]],
                                         axis=-1).astype(_BF16))
        w_gates = jnp.concatenate([wga, wgb], axis=-1).astype(_BF16)
        qkv1, qkv4, qkv16, q_t, k_b, v_t, k_decay, gates = _mixer_projection(
            x, sh2, sc2, norm_mix[layer], w_dil,
            jnp.concatenate([wqb * (inv_sqrt_e * LOG2_E), wvb], axis=-1).T.astype(_BF16),
            wkb.astype(_BF16), w_gates,
            _decay_columns(wf).astype(_BF16), _decay_columns(forget_bias[layer][None, :]))

        dil = [_dilated_attention(qkv, g) for g, qkv in enumerate((qkv1, qkv4, qkv16))]
        y_b = _forgetting_attention(q_t, k_b, k_decay, v_t)
        x = _merge(x, gt2, [o for o, _ in dil], [l for _, l in dil], y_b, gates,
                   w_branch_a[layer].astype(_BF16), w_branch_b[layer].astype(_BF16),
                   w_out[layer].astype(_BF16))

        last = layer == depth - 1
        x = _ffn(x, sh3, sc3, gt3, norm_ffn2[layer], ffn2_w_gate[layer], ffn2_w_up[layer],
                 ffn2_w_down[layer], final_gain=norm_final if last else None)
    return x
```

```python
import functools
import math

import numpy as np
import jax
import jax.numpy as jnp
from jax import lax
from jax.experimental import pallas as pl
from jax.experimental.pallas import tpu as pltpu

D_MODEL = 1024
HEAD_DIM = 64
DIL_GROUPS = ((128, 1), (512, 4), (2048, 16))
N_DIL = len(DIL_GROUPS)
DIL_HEADS_PER_GROUP = 4
DIL_HEADS = N_DIL * DIL_HEADS_PER_GROUP
DIL_WIDTH = DIL_HEADS * HEAD_DIM
DIL_OUT = DIL_HEADS_PER_GROUP * HEAD_DIM
FOX_HEADS = 8
FOX_PAIRS = FOX_HEADS // 2
FOX_WIDTH = FOX_HEADS * HEAD_DIM
BLOCK = 128
N_MOD = 9
RMS_EPS = 1e-6
ALIBI_MAX_BIAS = 8.0
NEG_INF = -1e30
IN_SIZES = (DIL_WIDTH, DIL_WIDTH, DIL_WIDTH, FOX_WIDTH, FOX_WIDTH, FOX_WIDTH,
            FOX_HEADS, D_MODEL, D_MODEL)

LANES = 128
SUBLANES = 8
VMEM_LIMIT_BYTES = 56 * 1024 * 1024

FFN_ROWS = 512
FFN_CHUNK = 256
PROJ_ROWS = 512
PROJ_CHUNK = 256
FOX_Q = 1024
FOX_K = 256
FOX_ACC_ROWS = HEAD_DIM + 16
DECAY_PIECES = 3
LOG2_E = math.log2(math.e)
DIL_BLOCKS_PER_STEP = 4
FOX_COLS = 512
ADA_COLS = 1024

_F32 = jnp.float32
_BF16 = jnp.bfloat16


def _const_spec(shape):
    zeros = (0,) * len(shape)
    return pl.BlockSpec(shape, lambda *_: zeros, pipeline_mode=pl.Buffered(1))


def _params(*semantics):
    return pltpu.CompilerParams(dimension_semantics=semantics,
                                vmem_limit_bytes=VMEM_LIMIT_BYTES)


def _modulated_norm(x, gain, shift, scale):
    ms = jnp.mean(x * x, axis=-1, keepdims=True)
    y = x * lax.rsqrt(ms + RMS_EPS) * gain
    return y * (1.0 + scale) + shift


def _dot_nt(a, b):
    return lax.dot_general(a, b, (((1,), (1,)), ((), ())), preferred_element_type=_F32)


def _ada_kernel(c_ref, w_ref, b_ref, o_ref):
    c = c_ref[...]
    act = c * jax.nn.sigmoid(c)
    o_ref[...] = jnp.dot(act, w_ref[...], preferred_element_type=_F32,
                         precision=lax.Precision.HIGHEST) + b_ref[...]


def _ada_modulation(c, w, b):
    bsz, d = c.shape
    n = w.shape[1]
    return pl.pallas_call(
        _ada_kernel,
        out_shape=jax.ShapeDtypeStruct((bsz, n), _F32),
        grid=(n // ADA_COLS,),
        in_specs=[pl.BlockSpec((bsz, d), lambda j: (0, 0)),
                  pl.BlockSpec((d, ADA_COLS), lambda j: (0, j)),
                  pl.BlockSpec((1, ADA_COLS), lambda j: (0, j))],
        out_specs=pl.BlockSpec((bsz, ADA_COLS), lambda j: (0, j)),
        compiler_params=_params("arbitrary"),
        name="ada_modulation",
    )(c, w, b.reshape(1, n))


def _ffn_kernel(*refs, n_chunks, final_norm):
    if final_norm:
        (x_ref, sh_ref, sc_ref, gt_ref, g_ref, wg_ref, wu_ref, wd_ref, gf_ref,
         o_ref, h_scr, acc_scr) = refs
    else:
        (x_ref, sh_ref, sc_ref, gt_ref, g_ref, wg_ref, wu_ref, wd_ref,
         o_ref, h_scr, acc_scr) = refs
    h = _modulated_norm(x_ref[...], g_ref[...], sh_ref[...], sc_ref[...])
    h_scr[...] = h.astype(_BF16)
    for j in range(n_chunks):
        hb = h_scr[...]
        cols = slice(j * FFN_CHUNK, (j + 1) * FFN_CHUNK)
        g = jnp.dot(hb, wg_ref[:, cols], preferred_element_type=_F32)
        u = jnp.dot(hb, wu_ref[:, cols], preferred_element_type=_F32)
        acc_scr[:, cols] = (g * jax.nn.sigmoid(g) * u).astype(_BF16)
    down = jnp.dot(acc_scr[...], wd_ref[...], preferred_element_type=_F32)
    out = x_ref[...] + 0.5 * gt_ref[...] * down
    if final_norm:
        ms = jnp.mean(out * out, axis=-1, keepdims=True)
        out = out * lax.rsqrt(ms + RMS_EPS) * gf_ref[...]
    o_ref[...] = out


def _ffn(x, shift, scale, gate, gain, w_gate, w_up, w_down, final_gain=None):
    bsz, seq, d = x.shape
    d_ff = w_gate.shape[1]
    n_chunks = d_ff // FFN_CHUNK
    wg = w_gate.astype(_BF16)
    wu = w_up.astype(_BF16)
    wd = w_down.astype(_BF16)
    row_spec = pl.BlockSpec((None, FFN_ROWS, d), lambda b, i: (b, i, 0))
    mod_spec = pl.BlockSpec((None, 1, d), lambda b, i: (b, 0, 0))
    in_specs = [row_spec, mod_spec, mod_spec, mod_spec, _const_spec((1, d)),
                _const_spec(wg.shape), _const_spec(wu.shape), _const_spec(wd.shape)]
    args = [x, shift, scale, gate, gain.reshape(1, d), wg, wu, wd]
    if final_gain is not None:
        in_specs.append(_const_spec((1, d)))
        args.append(final_gain.reshape(1, d))
    return pl.pallas_call(
        functools.partial(_ffn_kernel, n_chunks=n_chunks, final_norm=final_gain is not None),
        out_shape=jax.ShapeDtypeStruct(x.shape, _F32),
        grid=(bsz, seq // FFN_ROWS),
        in_specs=in_specs,
        out_specs=row_spec,
        scratch_shapes=[pltpu.VMEM((FFN_ROWS, d), _BF16), pltpu.VMEM((FFN_ROWS, d_ff), _BF16)],
        compiler_params=_params("parallel", "parallel"),
        name="ffn_final" if final_gain is not None else "ffn",
    )(*args)


def _row_cumsum(x):
    n = x.shape[0]
    row = lax.broadcasted_iota(jnp.int32, x.shape, 0)
    shift = 1
    while shift < n:
        x = x + jnp.where(row >= shift, pltpu.roll(x, shift, 0), 0.0)
        shift *= 2
    return x


def _proj_kernel(x_ref, sh_ref, sc_ref, g_ref, wd1_ref, wd4_ref, wd16_ref, wqv_ref, wk_ref,
                 wgate_ref, wf_ref, fb_ref,
                 od1_ref, od4_ref, od16_ref, oq_ref, ok_ref, ov_ref, okx_ref, og_ref,
                 hf_scr, h_scr, hp4_scr, hp16_scr, carry_scr):
    rows = x_ref.shape[0]

    @pl.when(pl.program_id(1) == 0)
    def _():
        carry_scr[...] = jnp.zeros_like(carry_scr)

    h = _modulated_norm(x_ref[...], g_ref[...], sh_ref[...], sc_ref[...])
    h_scr[...] = h.astype(_BF16)

    def dilated_group(dilation, hp_scr, w_ref, o_ref):
        n = rows // dilation
        for c in range(w_ref.shape[1] // PROJ_CHUNK):
            cols = slice(c * PROJ_CHUNK, (c + 1) * PROJ_CHUNK)
            y = jnp.dot(hp_scr[...], w_ref[:, cols], preferred_element_type=_F32).astype(_BF16)
            for r in range(dilation):
                o_ref[r, :, cols] = y[r * n:(r + 1) * n, :]

    dilated_group(1, h_scr, wd1_ref, od1_ref)

    qv_t = _dot_nt(wqv_ref[...], h_scr[...]).astype(_BF16)
    oq_ref[...] = qv_t[0:FOX_WIDTH, :]
    for kb in range(rows // FOX_K):
        ov_ref[kb] = qv_t[FOX_WIDTH:, kb * FOX_K:(kb + 1) * FOX_K]
    for c in range(FOX_WIDTH // PROJ_CHUNK):
        feat = slice(c * PROJ_CHUNK, (c + 1) * PROJ_CHUNK)
        ok_ref[:, feat] = jnp.dot(h_scr[...], wk_ref[:, feat],
                                  preferred_element_type=_F32).astype(_BF16)

    f = jnp.dot(h_scr[...], wf_ref[...], preferred_element_type=_F32) + fb_ref[...]
    log_sig = jnp.minimum(f, 0.0) - jnp.log1p(jnp.exp(-jnp.abs(f)))
    cum = _row_cumsum(log_sig) + carry_scr[0:1, :]
    carry_scr[...] = jnp.broadcast_to(cum[rows - 1:rows, :], carry_scr.shape)
    cum = cum * LOG2_E
    p1 = cum.astype(_BF16)
    r1 = cum - p1.astype(_F32)
    p2 = r1.astype(_BF16)
    p3 = (r1 - p2.astype(_F32)).astype(_BF16)
    lane = lax.broadcasted_iota(jnp.int32, cum.shape, 1)
    piece = lane % DECAY_PIECES
    pieces = jnp.where(piece == 0, p1.astype(_F32),
                       jnp.where(piece == 1, p2.astype(_F32), p3.astype(_F32)))
    okx_ref[...] = jnp.where(lane < DECAY_PIECES * FOX_HEADS, pieces, 0.0).astype(_BF16)

    for c in range(wgate_ref.shape[1] // PROJ_CHUNK):
        cols = slice(c * PROJ_CHUNK, (c + 1) * PROJ_CHUNK)
        y = jnp.dot(h_scr[...], wgate_ref[:, cols], preferred_element_type=_F32)
        og_ref[:, cols] = jax.nn.sigmoid(y).astype(_BF16)

    for cb in range(hf_scr.shape[0]):
        hf_scr[cb] = h[:, cb * LANES:(cb + 1) * LANES]
    for dilation, hp_scr, w_ref, o_ref in ((4, hp4_scr, wd4_ref, od4_ref),
                                           (16, hp16_scr, wd16_ref, od16_ref)):
        n = rows // dilation
        for r in range(dilation):
            for cb in range(hf_scr.shape[0]):
                hp_scr[r * n:(r + 1) * n, cb * LANES:(cb + 1) * LANES] = (
                    hf_scr[cb, pl.ds(r, n, stride=dilation), :].astype(_BF16))
        dilated_group(dilation, hp_scr, w_ref, o_ref)


def _mixer_projection(x, shift, scale, gain, w_dil, w_qv_t, w_k, w_gates, w_f, f_bias):
    bsz, seq, d = x.shape
    rows = PROJ_ROWS
    n_tiles = seq // rows
    tile = lambda b, i: (b, i, 0)
    row = lambda n: pl.BlockSpec((None, rows, n), tile)
    mod_spec = pl.BlockSpec((None, 1, d), lambda b, i: (b, 0, 0))
    dil_shapes, dil_specs = [], []
    for _, dilation in DIL_GROUPS:
        dil_shapes.append(jax.ShapeDtypeStruct((bsz, dilation, seq // dilation, 3 * DIL_OUT), _BF16))
        dil_specs.append(pl.BlockSpec((None, dilation, rows // dilation, 3 * DIL_OUT),
                                      lambda b, i: (b, 0, i, 0)))
    out_shape = (*dil_shapes,
                 jax.ShapeDtypeStruct((bsz, n_tiles, FOX_WIDTH, rows), _BF16),
                 jax.ShapeDtypeStruct((bsz, seq, FOX_WIDTH), _BF16),
                 jax.ShapeDtypeStruct((bsz, seq // FOX_K, FOX_WIDTH, FOX_K), _BF16),
                 jax.ShapeDtypeStruct((bsz, seq, LANES), _BF16),
                 jax.ShapeDtypeStruct((bsz, seq, w_gates.shape[1]), _BF16))
    out_specs = (*dil_specs,
                 pl.BlockSpec((None, None, FOX_WIDTH, rows), lambda b, i: (b, i, 0, 0)),
                 row(FOX_WIDTH),
                 pl.BlockSpec((None, rows // FOX_K, FOX_WIDTH, FOX_K), lambda b, i: (b, i, 0, 0)),
                 row(LANES),
                 row(w_gates.shape[1]))
    weights = [*w_dil, w_qv_t, w_k, w_gates, w_f, f_bias]
    return pl.pallas_call(
        _proj_kernel,
        out_shape=out_shape,
        grid=(bsz, n_tiles),
        in_specs=[row(d), mod_spec, mod_spec, _const_spec((1, d)),
                  *[_const_spec(w.shape) for w in weights]],
        out_specs=out_specs,
        scratch_shapes=[pltpu.VMEM((d // LANES, rows, LANES), _F32), pltpu.VMEM((rows, d), _BF16),
                        pltpu.VMEM((rows, d), _BF16), pltpu.VMEM((rows, d), _BF16),
                        pltpu.VMEM((SUBLANES, LANES), _F32)],
        compiler_params=_params("parallel", "arbitrary"),
        name="mixer_projection",
    )(x, shift, scale, gain.reshape(1, d), *weights)


def _dil_kernel(q_ref, kp_ref, kc_ref, vp_ref, vc_ref, bias_ref, o_ref, lse_ref,
                k_scr, v_scr, *, rows):
    first_tile = (pl.program_id(2) == 0).astype(jnp.int32)
    heads = range(DIL_HEADS_PER_GROUP)
    lane = lax.broadcasted_iota(jnp.int32, (BLOCK, DIL_OUT), 1)
    head_mask = [(lane >= h * HEAD_DIM) & (lane < (h + 1) * HEAD_DIM) for h in heads]

    def pick(stacked):
        out = stacked[0:BLOCK, :]
        for h in heads[1:]:
            out = jnp.where(head_mask[h], stacked[h * BLOCK:(h + 1) * BLOCK, :], out)
        return out

    blocks = [(res, c) for res in range(q_ref.shape[0]) for c in range(rows // BLOCK)]
    for res in range(q_ref.shape[0]):
        k_scr[res, 0:BLOCK, :] = kp_ref[res]
        k_scr[res, BLOCK:, :] = kc_ref[res]
        v_scr[res, 0:BLOCK, :] = vp_ref[res]
        v_scr[res, BLOCK:, :] = vc_ref[res]
    scores = []
    for res, c in blocks:
        q = q_ref[res, c * BLOCK:(c + 1) * BLOCK, :]
        q4 = jnp.concatenate([jnp.where(head_mask[h], q, jnp.zeros_like(q)) for h in heads], axis=0)
        bias = bias_ref[first_tile] if c == 0 else bias_ref[0]
        scores.append(_dot_nt(q4, k_scr[res, c * BLOCK:(c + 2) * BLOCK, :]) + bias)
    tops = [jnp.max(s, axis=1, keepdims=True) for s in scores]
    probs = [jnp.exp2(s - m) for s, m in zip(scores, tops)]
    sums = [jnp.sum(p, axis=1, keepdims=True) for p in probs]
    outs = [jnp.dot(p.astype(_BF16), v_scr[res, c * BLOCK:(c + 2) * BLOCK, :],
                    preferred_element_type=_F32)
            for (res, c), p in zip(blocks, probs)]
    for n, (res, c) in enumerate(blocks):
        denom = pick(jnp.broadcast_to(sums[n], outs[n].shape))
        top = pick(jnp.broadcast_to(tops[n], outs[n].shape))
        o_ref[res, c * BLOCK:(c + 1) * BLOCK, :] = (pick(outs[n]) / denom).astype(o_ref.dtype)
        lse_ref[res, c * BLOCK:(c + 1) * BLOCK, :] = (top + jnp.log2(denom)) * (1.0 / LOG2_E)


def _dilated_bias(group):
    window, dilation = DIL_GROUPS[group]
    n_back = window // dilation
    slopes = np.array([2.0 ** (-ALIBI_MAX_BIAS * (i + 1) / DIL_HEADS) for i in range(DIL_HEADS)],
                      dtype=np.float32).reshape(N_DIL, DIL_HEADS_PER_GROUP)[group]
    qi = np.arange(BLOCK)[:, None]
    ki = np.arange(2 * BLOCK)[None, :]
    rel = BLOCK + qi - ki
    valid = (rel >= 0) & (rel <= n_back)
    bias = -slopes[:, None, None] * (rel * dilation).astype(np.float32)[None]
    bias = (bias.astype(np.float64) * LOG2_E).astype(np.float32)
    slabs = [np.where(ok[None], bias, np.float32(NEG_INF)) for ok in (valid, valid & (ki >= BLOCK))]
    return jnp.asarray(np.stack(slabs).reshape(2, DIL_HEADS_PER_GROUP * BLOCK, 2 * BLOCK)
                       .astype(np.float32))


def _dilated_attention(qkv, group):
    bsz, dilation, sub_len, _ = qkv.shape
    rows = min(sub_len, DIL_BLOCKS_PER_STEP * BLOCK)
    n_tiles = sub_len // rows
    blocks_per_tile = rows // BLOCK
    n_res = min(dilation, DIL_BLOCKS_PER_STEP // blocks_per_tile)

    def cur(n):
        return pl.BlockSpec((None, n_res, rows, DIL_OUT), lambda b, r, i: (b, r, i, n))

    def prev(n):
        return pl.BlockSpec((None, n_res, BLOCK, DIL_OUT),
                            lambda b, r, i: (b, r, jnp.maximum(i * blocks_per_tile - 1, 0), n))

    out_spec = pl.BlockSpec((None, n_res, rows, DIL_OUT), lambda b, r, i: (b, r, i, 0))
    return pl.pallas_call(
        functools.partial(_dil_kernel, rows=rows),
        out_shape=(jax.ShapeDtypeStruct((bsz, dilation, sub_len, DIL_OUT), _BF16),
                   jax.ShapeDtypeStruct((bsz, dilation, sub_len, DIL_OUT), _F32)),
        grid=(bsz, dilation // n_res, n_tiles),
        in_specs=[cur(0), prev(1), cur(1), prev(2), cur(2),
                  _const_spec((2, DIL_HEADS_PER_GROUP * BLOCK, 2 * BLOCK))],
        out_specs=(out_spec, out_spec),
        scratch_shapes=[pltpu.VMEM((n_res, rows + BLOCK, DIL_OUT), _BF16),
                        pltpu.VMEM((n_res, rows + BLOCK, DIL_OUT), _BF16)],
        compiler_params=_params("parallel", "parallel", "parallel"),
        name=f"dilated_attention_d{dilation}",
    )(qkv, qkv, qkv, qkv, qkv, _dilated_bias(group))


def _fox_kernel(q_ref, k_ref, kx_ref, v_ref, o_ref, qa_scr, s_scr, acc_scr, m_scr):
    pair = pl.program_id(1)
    i = pl.program_id(2)
    q_tiles, _, tile = q_ref.shape
    nq = q_tiles * tile
    n_diag = nq // FOX_K
    tap = lax.broadcasted_iota(jnp.int32, (LANES, nq), 0)
    zeros = jnp.zeros((HEAD_DIM, tile), _BF16)
    for hd in range(2):
        for t in range(q_tiles):
            cols = slice(t * tile, (t + 1) * tile)
            qa_scr[hd, 0:HEAD_DIM, cols] = q_ref[t, 0:HEAD_DIM, :] if hd == 0 else zeros
            qa_scr[hd, HEAD_DIM:LANES, cols] = zeros if hd == 0 else q_ref[t, HEAD_DIM:LANES, :]
        first = DECAY_PIECES * (2 * pair + hd)
        qa_scr[hd, LANES:, :] = jnp.where((tap >= first) & (tap < first + DECAY_PIECES),
                                          -1.0, 0.0).astype(_BF16)
    acc_scr[...] = jnp.zeros_like(acc_scr)
    m_scr[...] = jnp.full_like(m_scr, NEG_INF)
    ones = jnp.ones((FOX_ACC_ROWS - HEAD_DIM, FOX_K), _BF16)

    def scores(j, lo):
        start = pl.multiple_of(j * FOX_K, FOX_K)
        k_aug = jnp.concatenate([k_ref[pl.ds(start, FOX_K), :], kx_ref[pl.ds(start, FOX_K), :]],
                                axis=1)
        return [jnp.dot(k_aug, qa_scr[hd, :, lo:nq], preferred_element_type=_F32)
                for hd in range(2)]

    def absorb(j, slot, lo, masked):
        v_t = v_ref[j]
        for hd in range(2):
            v_aug = jnp.concatenate([v_t[hd * HEAD_DIM:(hd + 1) * HEAD_DIM, :], ones], axis=0)
            for c0 in range(lo, nq, FOX_COLS):
                cols = slice(c0, min(c0 + FOX_COLS, nq))
                width = cols.stop - cols.start
                x = s_scr[slot, hd, :, cols]
                if masked and c0 == lo:
                    keep = (lax.broadcasted_iota(jnp.int32, (FOX_K, FOX_K), 0)
                            <= lax.broadcasted_iota(jnp.int32, (FOX_K, FOX_K), 1))
                    diag = jnp.where(keep, x[:, :FOX_K], NEG_INF)
                    x = jnp.concatenate([diag, x[:, FOX_K:]], axis=1) if width > FOX_K else diag
                m_old = m_scr[hd, 0:1, cols]
                m_new = jnp.maximum(m_old, jnp.max(x, axis=0, keepdims=True))
                p = jnp.exp2(x - m_new).astype(_BF16)
                alpha = jnp.exp2(m_old - m_new)
                pv = jnp.dot(v_aug, p, preferred_element_type=_F32)
                acc_scr[hd, :, cols] = alpha * acc_scr[hd, :, cols] + pv
                m_scr[hd, :, cols] = jnp.broadcast_to(m_new, (SUBLANES, width))

    def stash(slot, s, lo):
        for hd in range(2):
            s_scr[slot, hd, :, lo:nq] = s[hd]

    stash(0, scores(0, 0), 0)

    def body(t, carry):
        j = n_diag * t
        for d in range(n_diag):
            stash((d + 1) % 2, scores(j + d + 1, 0), 0)
            absorb(j + d, d % 2, 0, False)
        return carry

    lax.fori_loop(0, i, body, 0)
    j = n_diag * i
    for d in range(n_diag):
        if d + 1 < n_diag:
            stash((d + 1) % 2, scores(j + d + 1, (d + 1) * FOX_K), (d + 1) * FOX_K)
        absorb(j + d, d % 2, d * FOX_K, True)
    out_t = jnp.concatenate([acc_scr[hd, 0:HEAD_DIM, :] / acc_scr[hd, HEAD_DIM:HEAD_DIM + 1, :]
                             for hd in range(2)], axis=0)
    o_ref[...] = out_t.T.astype(o_ref.dtype)


def _forgetting_attention(q_t, k, k_decay, v_t):
    bsz, seq, _ = k.shape
    tile = q_t.shape[3]
    assert FOX_Q % tile == 0 and (FOX_Q // FOX_K) % 2 == 0 and v_t.shape[3] == FOX_K
    return pl.pallas_call(
        _fox_kernel,
        out_shape=jax.ShapeDtypeStruct((bsz, seq, FOX_WIDTH), _BF16),
        grid=(bsz, FOX_PAIRS, seq // FOX_Q),
        in_specs=[pl.BlockSpec((None, FOX_Q // tile, LANES, tile), lambda b, p, i: (b, i, p, 0)),
                  pl.BlockSpec((None, seq, LANES), lambda b, p, i: (b, 0, p)),
                  pl.BlockSpec((None, seq, LANES), lambda b, p, i: (b, 0, 0)),
                  pl.BlockSpec((None, seq // FOX_K, LANES, FOX_K), lambda b, p, i: (b, 0, p, 0))],
        out_specs=pl.BlockSpec((None, FOX_Q, LANES), lambda b, p, i: (b, i, p)),
        scratch_shapes=[pltpu.VMEM((2, 2 * LANES, FOX_Q), _BF16),
                        pltpu.VMEM((2, 2, FOX_K, FOX_Q), _F32),
                        pltpu.VMEM((2, FOX_ACC_ROWS, FOX_Q), _F32),
                        pltpu.VMEM((2, SUBLANES, FOX_Q), _F32)],
        compiler_params=_params("parallel", "parallel", "arbitrary"),
        name="forgetting_attention",
    )(q_t, k, k_decay, v_t)


def _merge_kernel(x_ref, gt_ref, o1_ref, o4_ref, o16_ref, l1_ref, l4_ref, l16_ref, yb_ref,
                  ga_ref, gb_ref, wa_ref, wb_ref, wo_ref, out_ref, o_scr, l_scr):
    rows = x_ref.shape[0]
    for g, (dilation, o_ref, l_ref) in enumerate(((4, o4_ref, l4_ref), (16, o16_ref, l16_ref))):
        n = rows // dilation
        for r in range(dilation):
            o_res = o_ref[r].astype(_F32)
            l_res = l_ref[r]
            for cb in range(DIL_OUT // LANES):
                lanes = slice(cb * LANES, (cb + 1) * LANES)
                o_scr[g, cb, pl.ds(r, n, stride=dilation), :] = o_res[:, lanes]
                l_scr[g, cb, pl.ds(r, n, stride=dilation), :] = l_res[:, lanes]

    def token_order(scr, g):
        return jnp.concatenate([scr[g, cb] for cb in range(DIL_OUT // LANES)], axis=1)

    l1, l2, l3 = l1_ref[0], token_order(l_scr, 0), token_order(l_scr, 1)
    top = jnp.maximum(jnp.maximum(l1, l2), l3)
    e1, e2, e3 = jnp.exp(l1 - top), jnp.exp(l2 - top), jnp.exp(l3 - top)
    y_a = (e1 * o1_ref[0].astype(_F32) + e2 * token_order(o_scr, 0)
           + e3 * token_order(o_scr, 1)) / (e1 + e2 + e3)
    br_a = jnp.dot(y_a.astype(_BF16), wa_ref[...], preferred_element_type=_F32)
    br_b = jnp.dot(yb_ref[...], wb_ref[...], preferred_element_type=_F32)
    merged = ga_ref[...].astype(_F32) * br_a + gb_ref[...].astype(_F32) * br_b
    mixed = jnp.dot(merged.astype(_BF16), wo_ref[...], preferred_element_type=_F32)
    out_ref[...] = x_ref[...] + gt_ref[...] * mixed


def _merge(x, gate, dil_outs, dil_lses, y_b, gates, w_branch_a, w_branch_b, w_out):
    bsz, seq, d = x.shape
    rows = PROJ_ROWS
    row = lambda n, c=0: pl.BlockSpec((None, rows, n), lambda b, i: (b, i, c))
    dil_specs = [pl.BlockSpec((None, dilation, rows // dilation, DIL_OUT), lambda b, i: (b, 0, i, 0))
                 for _, dilation in DIL_GROUPS]
    return pl.pallas_call(
        _merge_kernel,
        out_shape=jax.ShapeDtypeStruct(x.shape, _F32),
        grid=(bsz, seq // rows),
        in_specs=[row(d), pl.BlockSpec((None, 1, d), lambda b, i: (b, 0, 0)),
                  *dil_specs, *dil_specs,
                  row(FOX_WIDTH), row(d, 0), row(d, 1),
                  _const_spec(w_branch_a.shape), _const_spec(w_branch_b.shape),
                  _const_spec(w_out.shape)],
        out_specs=row(d),
        scratch_shapes=[pltpu.VMEM((N_DIL - 1, DIL_OUT // LANES, rows, LANES), _F32),
                        pltpu.VMEM((N_DIL - 1, DIL_OUT // LANES, rows, LANES), _F32)],
        compiler_params=_params("parallel", "parallel"),
        name="branch_merge",
    )(x, gate, *dil_outs, *dil_lses, y_b, gates, gates, w_branch_a, w_branch_b, w_out)


def _decay_columns(values):
    rep = jnp.repeat(values, DECAY_PIECES, axis=-1)
    pad = [(0, 0)] * (values.ndim - 1) + [(0, LANES - rep.shape[-1])]
    return jnp.pad(rep, pad)


def kernel(x, c, ada_w, ada_b, norm_ffn1, ffn1_w_gate, ffn1_w_up, ffn1_w_down, norm_mix, w_in,
           forget_bias, w_branch_a, w_branch_b, w_out, norm_ffn2, ffn2_w_gate, ffn2_w_up,
           ffn2_w_down, norm_final):
    bsz, seq, d = x.shape
    depth = ada_w.shape[0]
    if depth < 1:
        raise ValueError("depth must be positive")
    q_scale = LOG2_E / math.sqrt(HEAD_DIM)
    offsets = [int(o) for o in np.cumsum(IN_SIZES)[:-1]]
    for layer in range(depth):
        mod = _ada_modulation(c, ada_w[layer], ada_b[layer])
        sh1, sc1, gt1, sh2, sc2, gt2, sh3, sc3, gt3 = (
            m.reshape(bsz, 1, d) for m in jnp.split(mod, N_MOD, axis=-1))

        x = _ffn(x, sh1, sc1, gt1, norm_ffn1[layer], ffn1_w_gate[layer], ffn1_w_up[layer],
                 ffn1_w_down[layer])

        wqa, wka, wva, wqb, wkb, wvb, wf, wga, wgb = jnp.split(w_in[layer], offsets, axis=-1)
        w_dil = []
        for g in range(N_DIL):
            cols = slice(g * DIL_OUT, (g + 1) * DIL_OUT)
            w_dil.append(jnp.concatenate([wqa[:, cols] * q_scale, wka[:, cols], wva[:, cols]],
                                         axis=-1).astype(_BF16))
        w_gates = jnp.concatenate([wga, wgb], axis=-1).astype(_BF16)
        qkv1, qkv4, qkv16, q_t, k_b, v_t, k_decay, gates = _mixer_projection(
            x, sh2, sc2, norm_mix[layer], w_dil,
            jnp.concatenate([wqb * q_scale, wvb], axis=-1).T.astype(_BF16),
            wkb.astype(_BF16), w_gates,
            _decay_columns(wf).astype(_BF16), _decay_columns(forget_bias[layer][None, :]))

        dil = [_dilated_attention(qkv, g) for g, qkv in enumerate((qkv1, qkv4, qkv16))]
        y_b = _forgetting_attention(q_t, k_b, k_decay, v_t)
        x = _merge(x, gt2, [o for o, _ in dil], [l for _, l in dil], y_b, gates,
                   w_branch_a[layer].astype(_BF16), w_branch_b[layer].astype(_BF16),
                   w_out[layer].astype(_BF16))

        last = layer == depth - 1
        x = _ffn(x, sh3, sc3, gt3, norm_ffn2[layer], ffn2_w_gate[layer], ffn2_w_up[layer],
                 ffn2_w_down[layer], final_gain=norm_final if last else None)
    return x
```

```python
import functools
import math

import numpy as np
import jax
import jax.numpy as jnp
from jax import lax
from jax.experimental import pallas as pl
from jax.experimental.pallas import tpu as pltpu

D_MODEL = 1024
HEAD_DIM = 64
DIL_GROUPS = ((128, 1), (512, 4), (2048, 16))
N_DIL = len(DIL_GROUPS)
DIL_HEADS_PER_GROUP = 4
DIL_HEADS = N_DIL * DIL_HEADS_PER_GROUP
DIL_WIDTH = DIL_HEADS * HEAD_DIM
DIL_OUT = DIL_HEADS_PER_GROUP * HEAD_DIM
FOX_HEADS = 8
FOX_PAIRS = FOX_HEADS // 2
FOX_WIDTH = FOX_HEADS * HEAD_DIM
BLOCK = 128
N_MOD = 9
RMS_EPS = 1e-6
ALIBI_MAX_BIAS = 8.0
NEG_INF = -1e30
IN_SIZES = (DIL_WIDTH, DIL_WIDTH, DIL_WIDTH, FOX_WIDTH, FOX_WIDTH, FOX_WIDTH,
            FOX_HEADS, D_MODEL, D_MODEL)

LANES = 128
SUBLANES = 8
VMEM_LIMIT_BYTES = 56 * 1024 * 1024

FFN_ROWS = 512
FFN_CHUNK = 256
PROJ_ROWS = 512
PROJ_CHUNK = 256
FOX_Q = 1024
FOX_K = 256
FOX_ACC_ROWS = HEAD_DIM + 16
DECAY_PIECES = 3
LOG2_E = math.log2(math.e)
DIL_BLOCKS_PER_STEP = 4
FOX_COLS = 512
ADA_COLS = 1024

_F32 = jnp.float32
_BF16 = jnp.bfloat16


def _const_spec(shape):
    zeros = (0,) * len(shape)
    return pl.BlockSpec(shape, lambda *_: zeros, pipeline_mode=pl.Buffered(1))


def _params(*semantics):
    return pltpu.CompilerParams(dimension_semantics=semantics,
                                vmem_limit_bytes=VMEM_LIMIT_BYTES)


def _modulated_norm(x, gain, shift, scale):
    ms = jnp.mean(x * x, axis=-1, keepdims=True)
    y = x * lax.rsqrt(ms + RMS_EPS) * gain
    return y * (1.0 + scale) + shift


def _dot_nt(a, b):
    return lax.dot_general(a, b, (((1,), (1,)), ((), ())), preferred_element_type=_F32)


def _ada_kernel(c_ref, w_ref, b_ref, o_ref):
    c = c_ref[...]
    act = c * jax.nn.sigmoid(c)
    o_ref[...] = jnp.dot(act, w_ref[...], preferred_element_type=_F32,
                         precision=lax.Precision.HIGHEST) + b_ref[...]


def _ada_modulation(c, w, b):
    bsz, d = c.shape
    n = w.shape[1]
    return pl.pallas_call(
        _ada_kernel,
        out_shape=jax.ShapeDtypeStruct((bsz, n), _F32),
        grid=(n // ADA_COLS,),
        in_specs=[pl.BlockSpec((bsz, d), lambda j: (0, 0)),
                  pl.BlockSpec((d, ADA_COLS), lambda j: (0, j)),
                  pl.BlockSpec((1, ADA_COLS), lambda j: (0, j))],
        out_specs=pl.BlockSpec((bsz, ADA_COLS), lambda j: (0, j)),
        compiler_params=_params("arbitrary"),
        name="ada_modulation",
    )(c, w, b.reshape(1, n))


def _ffn_kernel(*refs, n_chunks, final_norm):
    if final_norm:
        (x_ref, sh_ref, sc_ref, gt_ref, g_ref, wg_ref, wu_ref, wd_ref, gf_ref,
         o_ref, h_scr, acc_scr) = refs
    else:
        (x_ref, sh_ref, sc_ref, gt_ref, g_ref, wg_ref, wu_ref, wd_ref,
         o_ref, h_scr, acc_scr) = refs
    h = _modulated_norm(x_ref[...], g_ref[...], sh_ref[...], sc_ref[...])
    h_scr[...] = h.astype(_BF16)
    for j in range(n_chunks):
        hb = h_scr[...]
        cols = slice(j * FFN_CHUNK, (j + 1) * FFN_CHUNK)
        g = jnp.dot(hb, wg_ref[:, cols], preferred_element_type=_F32)
        u = jnp.dot(hb, wu_ref[:, cols], preferred_element_type=_F32)
        acc_scr[:, cols] = (g * jax.nn.sigmoid(g) * u).astype(_BF16)
    down = jnp.dot(acc_scr[...], wd_ref[...], preferred_element_type=_F32)
    out = x_ref[...] + 0.5 * gt_ref[...] * down
    if final_norm:
        ms = jnp.mean(out * out, axis=-1, keepdims=True)
        out = out * lax.rsqrt(ms + RMS_EPS) * gf_ref[...]
    o_ref[...] = out


def _ffn(x, shift, scale, gate, gain, w_gate, w_up, w_down, final_gain=None):
    bsz, seq, d = x.shape
    d_ff = w_gate.shape[1]
    n_chunks = d_ff // FFN_CHUNK
    wg = w_gate.astype(_BF16)
    wu = w_up.astype(_BF16)
    wd = w_down.astype(_BF16)
    row_spec = pl.BlockSpec((None, FFN_ROWS, d), lambda b, i: (b, i, 0))
    mod_spec = pl.BlockSpec((None, 1, d), lambda b, i: (b, 0, 0))
    in_specs = [row_spec, mod_spec, mod_spec, mod_spec, _const_spec((1, d)),
                _const_spec(wg.shape), _const_spec(wu.shape), _const_spec(wd.shape)]
    args = [x, shift, scale, gate, gain.reshape(1, d), wg, wu, wd]
    if final_gain is not None:
        in_specs.append(_const_spec((1, d)))
        args.append(final_gain.reshape(1, d))
    return pl.pallas_call(
        functools.partial(_ffn_kernel, n_chunks=n_chunks, final_norm=final_gain is not None),
        out_shape=jax.ShapeDtypeStruct(x.shape, _F32),
        grid=(bsz, seq // FFN_ROWS),
        in_specs=in_specs,
        out_specs=row_spec,
        scratch_shapes=[pltpu.VMEM((FFN_ROWS, d), _BF16), pltpu.VMEM((FFN_ROWS, d_ff), _BF16)],
        compiler_params=_params("parallel", "parallel"),
        name="ffn_final" if final_gain is not None else "ffn",
    )(*args)


def _row_cumsum(x):
    n = x.shape[0]
    row = lax.broadcasted_iota(jnp.int32, x.shape, 0)
    shift = 1
    while shift < n:
        x = x + jnp.where(row >= shift, pltpu.roll(x, shift, 0), 0.0)
        shift *= 2
    return x


def _proj_kernel(x_ref, sh_ref, sc_ref, g_ref, wd1_ref, wd4_ref, wd16_ref, wqv_ref, wk_ref,
                 wf_ref, fb_ref,
                 od1_ref, od4_ref, od16_ref, oq_ref, ok_ref, ov_ref, okx_ref,
                 hf_scr, h_scr, hp4_scr, hp16_scr, carry_scr):
    rows = x_ref.shape[0]

    @pl.when(pl.program_id(1) == 0)
    def _():
        carry_scr[...] = jnp.zeros_like(carry_scr)

    h = _modulated_norm(x_ref[...], g_ref[...], sh_ref[...], sc_ref[...])
    h_scr[...] = h.astype(_BF16)

    def dilated_group(dilation, hp_scr, w_ref, o_ref):
        n = rows // dilation
        for c in range(w_ref.shape[1] // PROJ_CHUNK):
            cols = slice(c * PROJ_CHUNK, (c + 1) * PROJ_CHUNK)
            y = jnp.dot(hp_scr[...], w_ref[:, cols], preferred_element_type=_F32).astype(_BF16)
            for r in range(dilation):
                o_ref[r, :, cols] = y[r * n:(r + 1) * n, :]

    dilated_group(1, h_scr, wd1_ref, od1_ref)

    qv_t = _dot_nt(wqv_ref[...], h_scr[...]).astype(_BF16)
    oq_ref[...] = qv_t[0:FOX_WIDTH, :]
    for kb in range(rows // FOX_K):
        ov_ref[kb] = qv_t[FOX_WIDTH:, kb * FOX_K:(kb + 1) * FOX_K]
    for c in range(FOX_WIDTH // PROJ_CHUNK):
        feat = slice(c * PROJ_CHUNK, (c + 1) * PROJ_CHUNK)
        ok_ref[:, feat] = jnp.dot(h_scr[...], wk_ref[:, feat],
                                  preferred_element_type=_F32).astype(_BF16)

    f = jnp.dot(h_scr[...], wf_ref[...], preferred_element_type=_F32) + fb_ref[...]
    log_sig = jnp.minimum(f, 0.0) - jnp.log1p(jnp.exp(-jnp.abs(f)))
    cum = _row_cumsum(log_sig) + carry_scr[0:1, :]
    carry_scr[...] = jnp.broadcast_to(cum[rows - 1:rows, :], carry_scr.shape)
    cum = cum * LOG2_E
    p1 = cum.astype(_BF16)
    r1 = cum - p1.astype(_F32)
    p2 = r1.astype(_BF16)
    p3 = (r1 - p2.astype(_F32)).astype(_BF16)
    lane = lax.broadcasted_iota(jnp.int32, cum.shape, 1)
    piece = lane % DECAY_PIECES
    pieces = jnp.where(piece == 0, p1.astype(_F32),
                       jnp.where(piece == 1, p2.astype(_F32), p3.astype(_F32)))
    okx_ref[...] = jnp.where(lane < DECAY_PIECES * FOX_HEADS, pieces, 0.0).astype(_BF16)

    for cb in range(hf_scr.shape[0]):
        hf_scr[cb] = h[:, cb * LANES:(cb + 1) * LANES]
    for dilation, hp_scr, w_ref, o_ref in ((4, hp4_scr, wd4_ref, od4_ref),
                                           (16, hp16_scr, wd16_ref, od16_ref)):
        n = rows // dilation
        for r in range(dilation):
            for cb in range(hf_scr.shape[0]):
                hp_scr[r * n:(r + 1) * n, cb * LANES:(cb + 1) * LANES] = (
                    hf_scr[cb, pl.ds(r, n, stride=dilation), :].astype(_BF16))
        dilated_group(dilation, hp_scr, w_ref, o_ref)


def _mixer_projection(x, shift, scale, gain, w_dil, w_qv_t, w_k, w_f, f_bias):
    bsz, seq, d = x.shape
    rows = PROJ_ROWS
    n_tiles = seq // rows
    tile = lambda b, i: (b, i, 0)
    row = lambda n: pl.BlockSpec((None, rows, n), tile)
    mod_spec = pl.BlockSpec((None, 1, d), lambda b, i: (b, 0, 0))
    dil_shapes, dil_specs = [], []
    for _, dilation in DIL_GROUPS:
        dil_shapes.append(jax.ShapeDtypeStruct((bsz, dilation, seq // dilation, 3 * DIL_OUT), _BF16))
        dil_specs.append(pl.BlockSpec((None, dilation, rows // dilation, 3 * DIL_OUT),
                                      lambda b, i: (b, 0, i, 0)))
    out_shape = (*dil_shapes,
                 jax.ShapeDtypeStruct((bsz, n_tiles, FOX_WIDTH, rows), _BF16),
                 jax.ShapeDtypeStruct((bsz, seq, FOX_WIDTH), _BF16),
                 jax.ShapeDtypeStruct((bsz, seq // FOX_K, FOX_WIDTH, FOX_K), _BF16),
                 jax.ShapeDtypeStruct((bsz, seq, LANES), _BF16))
    out_specs = (*dil_specs,
                 pl.BlockSpec((None, None, FOX_WIDTH, rows), lambda b, i: (b, i, 0, 0)),
                 row(FOX_WIDTH),
                 pl.BlockSpec((None, rows // FOX_K, FOX_WIDTH, FOX_K), lambda b, i: (b, i, 0, 0)),
                 row(LANES))
    weights = [*w_dil, w_qv_t, w_k, w_f, f_bias]
    return pl.pallas_call(
        _proj_kernel,
        out_shape=out_shape,
        grid=(bsz, n_tiles),
        in_specs=[row(d), mod_spec, mod_spec, _const_spec((1, d)),
                  *[_const_spec(w.shape) for w in weights]],
        out_specs=out_specs,
        scratch_shapes=[pltpu.VMEM((d // LANES, rows, LANES), _F32), pltpu.VMEM((rows, d), _BF16),
                        pltpu.VMEM((rows, d), _BF16), pltpu.VMEM((rows, d), _BF16),
                        pltpu.VMEM((SUBLANES, LANES), _F32)],
        compiler_params=_params("parallel", "arbitrary"),
        name="mixer_projection",
    )(x, shift, scale, gain.reshape(1, d), *weights)


def _dil_kernel(q_ref, kp_ref, kc_ref, vp_ref, vc_ref, bias_ref, o_ref, lse_ref,
                k_scr, v_scr, *, rows):
    first_tile = (pl.program_id(2) == 0).astype(jnp.int32)
    heads = range(DIL_HEADS_PER_GROUP)
    lane = lax.broadcasted_iota(jnp.int32, (BLOCK, DIL_OUT), 1)
    head_mask = [(lane >= h * HEAD_DIM) & (lane < (h + 1) * HEAD_DIM) for h in heads]

    def pick(stacked):
        out = stacked[0:BLOCK, :]
        for h in heads[1:]:
            out = jnp.where(head_mask[h], stacked[h * BLOCK:(h + 1) * BLOCK, :], out)
        return out

    blocks = [(res, c) for res in range(q_ref.shape[0]) for c in range(rows // BLOCK)]
    for res in range(q_ref.shape[0]):
        k_scr[res, 0:BLOCK, :] = kp_ref[res]
        k_scr[res, BLOCK:, :] = kc_ref[res]
        v_scr[res, 0:BLOCK, :] = vp_ref[res]
        v_scr[res, BLOCK:, :] = vc_ref[res]
    scores = []
    for res, c in blocks:
        q = q_ref[res, c * BLOCK:(c + 1) * BLOCK, :]
        q4 = jnp.concatenate([jnp.where(head_mask[h], q, jnp.zeros_like(q)) for h in heads], axis=0)
        bias = bias_ref[first_tile] if c == 0 else bias_ref[0]
        scores.append(_dot_nt(q4, k_scr[res, c * BLOCK:(c + 2) * BLOCK, :]) + bias)
    tops = [jnp.max(s, axis=1, keepdims=True) for s in scores]
    probs = [jnp.exp2(s - m) for s, m in zip(scores, tops)]
    sums = [jnp.sum(p, axis=1, keepdims=True) for p in probs]
    outs = [jnp.dot(p.astype(_BF16), v_scr[res, c * BLOCK:(c + 2) * BLOCK, :],
                    preferred_element_type=_F32)
            for (res, c), p in zip(blocks, probs)]
    for n, (res, c) in enumerate(blocks):
        denom = pick(jnp.broadcast_to(sums[n], outs[n].shape))
        top = pick(jnp.broadcast_to(tops[n], outs[n].shape))
        o_ref[res, c * BLOCK:(c + 1) * BLOCK, :] = (pick(outs[n]) / denom).astype(o_ref.dtype)
        lse_ref[res, c * BLOCK:(c + 1) * BLOCK, :] = (top + jnp.log2(denom)) * (1.0 / LOG2_E)


def _dilated_bias(group):
    window, dilation = DIL_GROUPS[group]
    n_back = window // dilation
    slopes = np.array([2.0 ** (-ALIBI_MAX_BIAS * (i + 1) / DIL_HEADS) for i in range(DIL_HEADS)],
                      dtype=np.float32).reshape(N_DIL, DIL_HEADS_PER_GROUP)[group]
    qi = np.arange(BLOCK)[:, None]
    ki = np.arange(2 * BLOCK)[None, :]
    rel = BLOCK + qi - ki
    valid = (rel >= 0) & (rel <= n_back)
    bias = -slopes[:, None, None] * (rel * dilation).astype(np.float32)[None]
    bias = (bias.astype(np.float64) * LOG2_E).astype(np.float32)
    slabs = [np.where(ok[None], bias, np.float32(NEG_INF)) for ok in (valid, valid & (ki >= BLOCK))]
    return jnp.asarray(np.stack(slabs).reshape(2, DIL_HEADS_PER_GROUP * BLOCK, 2 * BLOCK)
                       .astype(np.float32))


def _dilated_attention(qkv, group):
    bsz, dilation, sub_len, _ = qkv.shape
    rows = min(sub_len, DIL_BLOCKS_PER_STEP * BLOCK)
    n_tiles = sub_len // rows
    blocks_per_tile = rows // BLOCK
    n_res = min(dilation, DIL_BLOCKS_PER_STEP // blocks_per_tile)

    def cur(n):
        return pl.BlockSpec((None, n_res, rows, DIL_OUT), lambda b, r, i: (b, r, i, n))

    def prev(n):
        return pl.BlockSpec((None, n_res, BLOCK, DIL_OUT),
                            lambda b, r, i: (b, r, jnp.maximum(i * blocks_per_tile - 1, 0), n))

    out_spec = pl.BlockSpec((None, n_res, rows, DIL_OUT), lambda b, r, i: (b, r, i, 0))
    return pl.pallas_call(
        functools.partial(_dil_kernel, rows=rows),
        out_shape=(jax.ShapeDtypeStruct((bsz, dilation, sub_len, DIL_OUT), _BF16),
                   jax.ShapeDtypeStruct((bsz, dilation, sub_len, DIL_OUT), _F32)),
        grid=(bsz, dilation // n_res, n_tiles),
        in_specs=[cur(0), prev(1), cur(1), prev(2), cur(2),
                  _const_spec((2, DIL_HEADS_PER_GROUP * BLOCK, 2 * BLOCK))],
        out_specs=(out_spec, out_spec),
        scratch_shapes=[pltpu.VMEM((n_res, rows + BLOCK, DIL_OUT), _BF16),
                        pltpu.VMEM((n_res, rows + BLOCK, DIL_OUT), _BF16)],
        compiler_params=_params("parallel", "parallel", "parallel"),
        name=f"dilated_attention_d{dilation}",
    )(qkv, qkv, qkv, qkv, qkv, _dilated_bias(group))


def _fox_kernel(q_ref, k_ref, kx_ref, v_ref, o_ref, qa_scr, s_scr, acc_scr, m_scr):
    pair = pl.program_id(1)
    i = pl.program_id(2)
    q_tiles, _, tile = q_ref.shape
    nq = q_tiles * tile
    n_diag = nq // FOX_K
    tap = lax.broadcasted_iota(jnp.int32, (LANES, nq), 0)
    zeros = jnp.zeros((HEAD_DIM, tile), _BF16)
    for hd in range(2):
        for t in range(q_tiles):
            cols = slice(t * tile, (t + 1) * tile)
            qa_scr[hd, 0:HEAD_DIM, cols] = q_ref[t, 0:HEAD_DIM, :] if hd == 0 else zeros
            qa_scr[hd, HEAD_DIM:LANES, cols] = zeros if hd == 0 else q_ref[t, HEAD_DIM:LANES, :]
        first = DECAY_PIECES * (2 * pair + hd)
        qa_scr[hd, LANES:, :] = jnp.where((tap >= first) & (tap < first + DECAY_PIECES),
                                          -1.0, 0.0).astype(_BF16)
    acc_scr[...] = jnp.zeros_like(acc_scr)
    m_scr[...] = jnp.full_like(m_scr, NEG_INF)
    ones = jnp.ones((FOX_ACC_ROWS - HEAD_DIM, FOX_K), _BF16)

    def scores(j, lo):
        start = pl.multiple_of(j * FOX_K, FOX_K)
        k_aug = jnp.concatenate([k_ref[pl.ds(start, FOX_K), :], kx_ref[pl.ds(start, FOX_K), :]],
                                axis=1)
        return [jnp.dot(k_aug, qa_scr[hd, :, lo:nq], preferred_element_type=_F32)
                for hd in range(2)]

    def absorb(j, slot, lo, masked):
        v_t = v_ref[j]
        for hd in range(2):
            v_aug = jnp.concatenate([v_t[hd * HEAD_DIM:(hd + 1) * HEAD_DIM, :], ones], axis=0)
            for c0 in range(lo, nq, FOX_COLS):
                cols = slice(c0, min(c0 + FOX_COLS, nq))
                width = cols.stop - cols.start
                x = s_scr[slot, hd, :, cols]
                if masked and c0 == lo:
                    keep = (lax.broadcasted_iota(jnp.int32, (FOX_K, FOX_K), 0)
                            <= lax.broadcasted_iota(jnp.int32, (FOX_K, FOX_K), 1))
                    diag = jnp.where(keep, x[:, :FOX_K], NEG_INF)
                    x = jnp.concatenate([diag, x[:, FOX_K:]], axis=1) if width > FOX_K else diag
                m_old = m_scr[hd, 0:1, cols]
                m_new = jnp.maximum(m_old, jnp.max(x, axis=0, keepdims=True))
                p = jnp.exp2(x - m_new).astype(_BF16)
                alpha = jnp.exp2(m_old - m_new)
                pv = jnp.dot(v_aug, p, preferred_element_type=_F32)
                acc_scr[hd, :, cols] = alpha * acc_scr[hd, :, cols] + pv
                m_scr[hd, :, cols] = jnp.broadcast_to(m_new, (SUBLANES, width))

    def stash(slot, s, lo):
        for hd in range(2):
            s_scr[slot, hd, :, lo:nq] = s[hd]

    stash(0, scores(0, 0), 0)

    def body(t, carry):
        j = n_diag * t
        for d in range(n_diag):
            stash((d + 1) % 2, scores(j + d + 1, 0), 0)
            absorb(j + d, d % 2, 0, False)
        return carry

    lax.fori_loop(0, i, body, 0)
    j = n_diag * i
    for d in range(n_diag):
        if d + 1 < n_diag:
            stash((d + 1) % 2, scores(j + d + 1, (d + 1) * FOX_K), (d + 1) * FOX_K)
        absorb(j + d, d % 2, d * FOX_K, True)
    out_t = jnp.concatenate([acc_scr[hd, 0:HEAD_DIM, :] / acc_scr[hd, HEAD_DIM:HEAD_DIM + 1, :]
                             for hd in range(2)], axis=0)
    o_ref[...] = out_t.T.astype(o_ref.dtype)


def _forgetting_attention(q_t, k, k_decay, v_t):
    bsz, seq, _ = k.shape
    tile = q_t.shape[3]
    assert FOX_Q % tile == 0 and (FOX_Q // FOX_K) % 2 == 0 and v_t.shape[3] == FOX_K
    return pl.pallas_call(
        _fox_kernel,
        out_shape=jax.ShapeDtypeStruct((bsz, seq, FOX_WIDTH), _BF16),
        grid=(bsz, FOX_PAIRS, seq // FOX_Q),
        in_specs=[pl.BlockSpec((None, FOX_Q // tile, LANES, tile), lambda b, p, i: (b, i, p, 0)),
                  pl.BlockSpec((None, seq, LANES), lambda b, p, i: (b, 0, p)),
                  pl.BlockSpec((None, seq, LANES), lambda b, p, i: (b, 0, 0)),
                  pl.BlockSpec((None, seq // FOX_K, LANES, FOX_K), lambda b, p, i: (b, 0, p, 0))],
        out_specs=pl.BlockSpec((None, FOX_Q, LANES), lambda b, p, i: (b, i, p)),
        scratch_shapes=[pltpu.VMEM((2, 2 * LANES, FOX_Q), _BF16),
                        pltpu.VMEM((2, 2, FOX_K, FOX_Q), _F32),
                        pltpu.VMEM((2, FOX_ACC_ROWS, FOX_Q), _F32),
                        pltpu.VMEM((2, SUBLANES, FOX_Q), _F32)],
        compiler_params=_params("parallel", "parallel", "arbitrary"),
        name="forgetting_attention",
    )(q_t, k, k_decay, v_t)


def _merge_kernel(x_ref, sh_ref, sc_ref, gt_ref, g_ref, o1_ref, o4_ref, o16_ref,
                  l1_ref, l4_ref, l16_ref, yb_ref, wgate_ref, wa_ref, wb_ref, wo_ref, out_ref,
                  h_scr, o_scr, l_scr):
    rows, d = x_ref.shape
    h_scr[...] = _modulated_norm(x_ref[...], g_ref[...], sh_ref[...], sc_ref[...]).astype(_BF16)
    gate_a = jax.nn.sigmoid(jnp.dot(h_scr[...], wgate_ref[:, 0:d], preferred_element_type=_F32))
    gate_b = jax.nn.sigmoid(jnp.dot(h_scr[...], wgate_ref[:, d:2 * d], preferred_element_type=_F32))
    for g, (dilation, o_ref, l_ref) in enumerate(((4, o4_ref, l4_ref), (16, o16_ref, l16_ref))):
        n = rows // dilation
        for r in range(dilation):
            o_res = o_ref[r].astype(_F32)
            l_res = l_ref[r]
            for cb in range(DIL_OUT // LANES):
                lanes = slice(cb * LANES, (cb + 1) * LANES)
                o_scr[g, cb, pl.ds(r, n, stride=dilation), :] = o_res[:, lanes]
                l_scr[g, cb, pl.ds(r, n, stride=dilation), :] = l_res[:, lanes]

    def token_order(scr, g):
        return jnp.concatenate([scr[g, cb] for cb in range(DIL_OUT // LANES)], axis=1)

    l1, l2, l3 = l1_ref[0], token_order(l_scr, 0), token_order(l_scr, 1)
    top = jnp.maximum(jnp.maximum(l1, l2), l3)
    e1, e2, e3 = jnp.exp(l1 - top), jnp.exp(l2 - top), jnp.exp(l3 - top)
    y_a = (e1 * o1_ref[0].astype(_F32) + e2 * token_order(o_scr, 0)
           + e3 * token_order(o_scr, 1)) / (e1 + e2 + e3)
    br_a = jnp.dot(y_a.astype(_BF16), wa_ref[...], preferred_element_type=_F32)
    br_b = jnp.dot(yb_ref[...], wb_ref[...], preferred_element_type=_F32)
    merged = gate_a * br_a + gate_b * br_b
    mixed = jnp.dot(merged.astype(_BF16), wo_ref[...], preferred_element_type=_F32)
    out_ref[...] = x_ref[...] + gt_ref[...] * mixed


def _merge(x, shift, scale, gate, gain, dil_outs, dil_lses, y_b, w_gates, w_branch_a, w_branch_b,
           w_out):
    bsz, seq, d = x.shape
    rows = PROJ_ROWS
    row = lambda n: pl.BlockSpec((None, rows, n), lambda b, i: (b, i, 0))
    mod_spec = pl.BlockSpec((None, 1, d), lambda b, i: (b, 0, 0))
    dil_specs = [pl.BlockSpec((None, dilation, rows // dilation, DIL_OUT), lambda b, i: (b, 0, i, 0))
                 for _, dilation in DIL_GROUPS]
    return pl.pallas_call(
        _merge_kernel,
        out_shape=jax.ShapeDtypeStruct(x.shape, _F32),
        grid=(bsz, seq // rows),
        in_specs=[row(d), mod_spec, mod_spec, mod_spec, _const_spec((1, d)),
                  *dil_specs, *dil_specs, row(FOX_WIDTH),
                  _const_spec(w_gates.shape), _const_spec(w_branch_a.shape),
                  _const_spec(w_branch_b.shape), _const_spec(w_out.shape)],
        out_specs=row(d),
        scratch_shapes=[pltpu.VMEM((rows, d), _BF16),
                        pltpu.VMEM((N_DIL - 1, DIL_OUT // LANES, rows, LANES), _F32),
                        pltpu.VMEM((N_DIL - 1, DIL_OUT // LANES, rows, LANES), _F32)],
        compiler_params=_params("parallel", "parallel"),
        name="branch_merge",
    )(x, shift, scale, gate, gain.reshape(1, d), *dil_outs, *dil_lses, y_b, w_gates,
      w_branch_a, w_branch_b, w_out)


def _decay_columns(values):
    rep = jnp.repeat(values, DECAY_PIECES, axis=-1)
    pad = [(0, 0)] * (values.ndim - 1) + [(0, LANES - rep.shape[-1])]
    return jnp.pad(rep, pad)


def kernel(x, c, ada_w, ada_b, norm_ffn1, ffn1_w_gate, ffn1_w_up, ffn1_w_down, norm_mix, w_in,
           forget_bias, w_branch_a, w_branch_b, w_out, norm_ffn2, ffn2_w_gate, ffn2_w_up,
           ffn2_w_down, norm_final):
    bsz, seq, d = x.shape
    depth = ada_w.shape[0]
    if depth < 1:
        raise ValueError("depth must be positive")
    q_scale = LOG2_E / math.sqrt(HEAD_DIM)
    offsets = [int(o) for o in np.cumsum(IN_SIZES)[:-1]]
    for layer in range(depth):
        mod = _ada_modulation(c, ada_w[layer], ada_b[layer])
        sh1, sc1, gt1, sh2, sc2, gt2, sh3, sc3, gt3 = (
            m.reshape(bsz, 1, d) for m in jnp.split(mod, N_MOD, axis=-1))

        x = _ffn(x, sh1, sc1, gt1, norm_ffn1[layer], ffn1_w_gate[layer], ffn1_w_up[layer],
                 ffn1_w_down[layer])

        wqa, wka, wva, wqb, wkb, wvb, wf, wga, wgb = jnp.split(w_in[layer], offsets, axis=-1)
        w_dil = []
        for g in range(N_DIL):
            cols = slice(g * DIL_OUT, (g + 1) * DIL_OUT)
            w_dil.append(jnp.concatenate([wqa[:, cols] * q_scale, wka[:, cols], wva[:, cols]],
                                         axis=-1).astype(_BF16))
        w_gates = jnp.concatenate([wga, wgb], axis=-1).astype(_BF16)
        qkv1, qkv4, qkv16, q_t, k_b, v_t, k_decay = _mixer_projection(
            x, sh2, sc2, norm_mix[layer], w_dil,
            jnp.concatenate([wqb * q_scale, wvb], axis=-1).T.astype(_BF16),
            wkb.astype(_BF16),
            _decay_columns(wf).astype(_BF16), _decay_columns(forget_bias[layer][None, :]))

        dil = [_dilated_attention(qkv, g) for g, qkv in enumerate((qkv1, qkv4, qkv16))]
        y_b = _forgetting_attention(q_t, k_b, k_decay, v_t)
        x = _merge(x, sh2, sc2, gt2, norm_mix[layer], [o for o, _ in dil], [l for _, l in dil],
                   y_b, w_gates, w_branch_a[layer].astype(_BF16),
                   w_branch_b[layer].astype(_BF16), w_out[layer].astype(_BF16))

        last = layer == depth - 1
        x = _ffn(x, sh3, sc3, gt3, norm_ffn2[layer], ffn2_w_gate[layer], ffn2_w_up[layer],
                 ffn2_w_down[layer], final_gain=norm_final if last else None)
    return x
```
